```python
import math
import jax
import jax.numpy as jnp
from jax import lax
import numpy as np

D_MODEL = 1024
BATCH = 8
SEQ = 4096
DEPTH = 4

GRID_W = 64
CTX_LEN = 256
N_GROUPS = 4
GROUP_W = D_MODEL // N_GROUPS
MIX_W = N_GROUPS * GROUP_W
N_H = 4
HEAD_DIM = GROUP_W // N_H
ROT_DIM = HEAD_DIM // 2
ROPE_THETA = 10000.0
Q_BLOCK = 128
DIFF_QK = HEAD_DIM // 2
MLA_Q_RANK = D_MODEL // 4
MLA_KV_RANK = D_MODEL // 8
MLA_NOPE = HEAD_DIM
MLA_ROPE = ROT_DIM
GLA_DK = HEAD_DIM // 2
GLA_GATE_RANK = 16
GLA_GATE_NORM = 16.0
RET_DK = HEAD_DIM // 2
CHUNK = 64
FFN_HIDDEN = -(-8 * D_MODEL // (3 * 256)) * 256
RMS_EPS = 1e-6

IN_SIZES = (
    N_H * 2 * DIFF_QK, N_H * 2 * DIFF_QK, N_H * HEAD_DIM,
    MLA_Q_RANK, MLA_KV_RANK, MLA_ROPE,
    N_H * GLA_DK, N_H * GLA_DK, N_H * HEAD_DIM, N_H * HEAD_DIM,
    GLA_GATE_RANK, GLA_GATE_RANK,
    N_H * RET_DK, N_H * RET_DK, N_H * HEAD_DIM, N_H * HEAD_DIM,
)
IN_WIDTH = sum(IN_SIZES)
IN_SPLITS = tuple(int(s) for s in np.cumsum(IN_SIZES)[:-1])

kernel_name = 'hybrid_parallel_group_dit_trunk'


def rms_norm(x, g):
    xf = x.astype(jnp.float32)
    y = xf * lax.rsqrt(jnp.mean(xf * xf, axis=-1, keepdims=True) + RMS_EPS)
    return (y * g.astype(jnp.float32)).astype(x.dtype)


def modulate(h, shift, scale):
    return h * (1.0 + scale) + shift


def axial_rope(rows, rot_dim):
    row = jnp.repeat(jnp.arange(rows, dtype=jnp.float32), GRID_W)
    col = jnp.tile(jnp.arange(GRID_W, dtype=jnp.float32), rows)
    n_freq = rot_dim // 4
    freqs = ROPE_THETA ** (-jnp.arange(n_freq, dtype=jnp.float32) / n_freq)
    ang = jnp.concatenate([row[:, None] * freqs, col[:, None] * freqs], axis=-1)
    return jnp.cos(ang), jnp.sin(ang)


def apply_rope(t, cos, sin):
    t1, t2 = jnp.split(t, 2, axis=-1)
    c, s = cos[:, None, :], sin[:, None, :]
    return jnp.concatenate([t1 * c - t2 * s, t1 * s + t2 * c], axis=-1).astype(t.dtype)


def attend(q, k, v, map_w):
    s = jnp.einsum('bqhmd,bkhmd->bhmqk', q, k).astype(jnp.float32)
    p = jax.nn.softmax(s, axis=-1)
    w = jnp.einsum('bhmqk,mh->bhqk', p, map_w.astype(jnp.float32))
    return jnp.einsum('bhqk,bkhd->bqhd', w.astype(v.dtype), v)


def blocked_attend(q, k, v, map_w):
    B, Lq = q.shape[:2]
    qb = q.reshape((B, Lq // Q_BLOCK, Q_BLOCK) + q.shape[2:]).swapaxes(0, 1)
    ob = lax.map(lambda qi: attend(qi, k, v, map_w), qb)
    return ob.swapaxes(0, 1).reshape((B, Lq) + ob.shape[3:])


def _chunks(t):
    B, L, H, d = t.shape
    return t.reshape(B, L // CHUNK, CHUNK, H, d).transpose(1, 0, 3, 2, 4)


def _unchunk(t):
    n, B, H, C, d = t.shape
    return t.transpose(1, 0, 3, 2, 4).reshape(B, n * C, H, d)


def gla_chunk_scan(q, k, v, log_a, state0):
    causal = jnp.tril(jnp.ones((CHUNK, CHUNK), dtype=bool))[:, :, None]

    def step(S, blk):
        qc, kc, vc, ac = blk
        b = jnp.cumsum(ac, axis=2)
        rel = jnp.where(causal, b[:, :, :, None, :] - b[:, :, None, :, :], -jnp.inf)
        scores = jnp.einsum('bhid,bhjd,bhijd->bhij', qc, kc, jnp.exp(rel))
        b_end = b[:, :, -1:, :]
        o = (jnp.einsum('bhij,bhje->bhie', scores, vc)
             + jnp.einsum('bhid,bhde->bhie', qc * jnp.exp(b), S))
        S_new = (jnp.exp(b_end)[:, :, 0, :, None] * S
                 + jnp.einsum('bhjd,bhje->bhde', kc * jnp.exp(b_end - b), vc))
        return S_new, o

    S_fin, out = lax.scan(step, state0, (_chunks(q), _chunks(k), _chunks(v), _chunks(log_a)))
    return _unchunk(out).astype(v.dtype), S_fin


def retention_chunk_scan(q, k, v, log_gamma, state0):
    pos = jnp.arange(CHUNK, dtype=jnp.float32)
    rel = pos[:, None] - pos[None, :]
    decay_in = jnp.exp(jnp.where(rel >= 0, rel * log_gamma[:, None, None], -jnp.inf))
    decay_q = jnp.exp((pos[None, :] + 1.0) * log_gamma[:, None])[None, :, :, None]
    decay_k = jnp.exp((CHUNK - 1.0 - pos[None, :]) * log_gamma[:, None])[None, :, :, None]
    decay_chunk = jnp.exp(CHUNK * log_gamma)[None, :, None, None]

    def step(S, blk):
        qc, kc, vc = blk
        inner = jnp.einsum('bhid,bhjd->bhij', qc, kc) * decay_in
        o = (jnp.einsum('bhij,bhje->bhie', inner, vc)
             + jnp.einsum('bhid,bhde->bhie', qc, S) * decay_q)
        S_new = decay_chunk * S + jnp.einsum('bhjd,bhje->bhde', kc * decay_k, vc)
        return S_new, o

    S_fin, out = lax.scan(step, state0, (_chunks(q), _chunks(k), _chunks(v)))
    return _unchunk(out).astype(v.dtype), S_fin


def bidirectional_scan(scan, ctx_in, lat_in, with_ctx):
    def rev(t):
        return jnp.flip(t, 1) if t.ndim == 4 else t
    qc, kc, vc, dcf, dcb = ctx_in
    ql, kl, vl, dlf, dlb = lat_in
    s0 = jnp.zeros((qc.shape[0], qc.shape[2], qc.shape[3], vc.shape[3]), jnp.float32)
    oc_f, sc_f = scan(qc, kc, vc, dcf, s0)
    oc_b, sc_b = scan(rev(qc), rev(kc), rev(vc), rev(dcb), s0)
    ol_f, _ = scan(ql, kl, vl, dlf, sc_f)
    ol_b, _ = scan(rev(ql), rev(kl), rev(vl), rev(dlb), sc_b)
    o_ctx = (oc_f + rev(oc_b)) if with_ctx else None
    return o_ctx, ol_f + rev(ol_b)


def diff_attention(p_ctx, p_lat, rope, lam_vecs, sub_g, lambda_init, with_ctx):
    def prep(q, k, v, rotate):
        B, L = q.shape[:2]
        q = q.reshape(B, L, 2 * N_H, DIFF_QK)
        k = k.reshape(B, L, 2 * N_H, DIFF_QK)
        if rotate:
            q, k = apply_rope(q, *rope), apply_rope(k, *rope)
        return (q.reshape(B, L, N_H, 2, DIFF_QK) * DIFF_QK ** -0.5,
                k.reshape(B, L, N_H, 2, DIFF_QK),
                v.reshape(B, L, N_H, HEAD_DIM))

    lq1, lk1, lq2, lk2 = lam_vecs.astype(jnp.float32)
    lam = jnp.exp(jnp.sum(lq1 * lk1)) - jnp.exp(jnp.sum(lq2 * lk2)) + lambda_init
    map_w = jnp.stack([jnp.ones((N_H,), jnp.float32), jnp.broadcast_to(-lam, (N_H,))])

    def finish(o):
        return (rms_norm(o, sub_g) * (1.0 - lambda_init)).reshape(o.shape[0], o.shape[1], GROUP_W)

    qc, kc, vc = prep(*p_ctx, False)
    ql, kl, vl = prep(*p_lat, True)
    o_lat = blocked_attend(ql, jnp.concatenate([kc, kl], 1), jnp.concatenate([vc, vl], 1), map_w)
    o_ctx = finish(attend(qc, kc, vc, map_w)) if with_ctx else None
    return o_ctx, finish(o_lat)


def latent_attention(p_ctx, p_lat, rope, q_g, kv_g, w_uq, w_ukv, with_ctx):
    scale = (MLA_NOPE + MLA_ROPE) ** -0.5

    def prep(q_c, kv_c, k_r, rotate):
        B, L = q_c.shape[:2]
        q = (rms_norm(q_c, q_g) @ w_uq).reshape(B, L, N_H, MLA_NOPE + MLA_ROPE)
        kv = (rms_norm(kv_c, kv_g) @ w_ukv).reshape(B, L, N_H, MLA_NOPE + HEAD_DIM)
        q_nope, q_rope = q[..., :MLA_NOPE], q[..., MLA_NOPE:]
        k_nope, v = kv[..., :MLA_NOPE], kv[..., MLA_NOPE:]
        k_rope = k_r[:, :, None, :]
        if rotate:
            q_rope, k_rope = apply_rope(q_rope, *rope), apply_rope(k_rope, *rope)
        k_rope = jnp.broadcast_to(k_rope, (B, L, N_H, MLA_ROPE))
        q = jnp.concatenate([q_nope, q_rope], -1)[:, :, :, None, :] * scale
        k = jnp.concatenate([k_nope, k_rope], -1)[:, :, :, None, :]
        return q, k, v

    map_w = jnp.ones((1, N_H), jnp.float32)
    qc, kc, vc = prep(*p_ctx, False)
    ql, kl, vl = prep(*p_lat, True)
    o_lat = blocked_attend(ql, jnp.concatenate([kc, kl], 1), jnp.concatenate([vc, vl], 1), map_w)

    def flat(o):
        return o.reshape(o.shape[0], o.shape[1], GROUP_W)
    o_ctx = flat(attend(qc, kc, vc, map_w)) if with_ctx else None
    return o_ctx, flat(o_lat)


def gated_linear_attention(p_ctx, p_lat, w_gate, b_gate, out_g, with_ctx):
    def prep(q, k, v, r, g_f, g_b):
        B, L = q.shape[:2]
        gates = [jax.nn.log_sigmoid((g @ w_gate[d] + b_gate[d]).astype(jnp.float32))
                 .reshape(B, L, N_H, GLA_DK) / GLA_GATE_NORM for d, g in enumerate((g_f, g_b))]
        return (q.reshape(B, L, N_H, GLA_DK) * GLA_DK ** -0.5, k.reshape(B, L, N_H, GLA_DK),
                v.reshape(B, L, N_H, HEAD_DIM), gates[0], gates[1]), r

    def finish(o, r):
        B, L = o.shape[:2]
        return (rms_norm(o, out_g) * jax.nn.silu(r).reshape(B, L, N_H, HEAD_DIM)).reshape(B, L, GROUP_W)

    ctx_in, r_c = prep(*p_ctx)
    lat_in, r_l = prep(*p_lat)
    o_ctx, o_lat = bidirectional_scan(gla_chunk_scan, ctx_in, lat_in, with_ctx)
    return (finish(o_ctx, r_c) if with_ctx else None), finish(o_lat, r_l)


def retention(p_ctx, p_lat, rope, decay_logits, out_g, with_ctx):
    log_gamma = jax.nn.log_sigmoid(decay_logits.astype(jnp.float32))

    def prep(q, k, v, g, rotate):
        B, L = q.shape[:2]
        q = q.reshape(B, L, N_H, RET_DK)
        k = k.reshape(B, L, N_H, RET_DK) * RET_DK ** -0.5
        if rotate:
            q, k = apply_rope(q, *rope), apply_rope(k, *rope)
        return (q, k, v.reshape(B, L, N_H, HEAD_DIM), log_gamma[0], log_gamma[1]), g

    def finish(o, g):
        B, L = o.shape[:2]
        return (rms_norm(o, out_g) * jax.nn.silu(g).reshape(B, L, N_H, HEAD_DIM)).reshape(B, L, GROUP_W)

    ctx_in, g_c = prep(*p_ctx, False)
    lat_in, g_l = prep(*p_lat, True)
    o_ctx, o_lat = bidirectional_scan(retention_chunk_scan, ctx_in, lat_in, with_ctx)
    return (finish(o_ctx, g_c) if with_ctx else None), finish(o_lat, g_l)


def hybrid_mixer(hc, hl, rope, lambda_init, with_ctx, w_in, diff_lam, diff_norm,
                 mla_q_norm, mla_kv_norm, mla_w_uq, mla_w_ukv,
                 gla_w_gate, gla_b_gate, gla_norm, ret_decay, ret_norm, w_out):
    pc = jnp.split(hc @ w_in, IN_SPLITS, axis=-1)
    pl = jnp.split(hl @ w_in, IN_SPLITS, axis=-1)
    dc, dl = diff_attention(pc[0:3], pl[0:3], rope, diff_lam, diff_norm, lambda_init, with_ctx)
    mc, ml = latent_attention(pc[3:6], pl[3:6], rope, mla_q_norm, mla_kv_norm, mla_w_uq, mla_w_ukv, with_ctx)
    gc, gl = gated_linear_attention(pc[6:12], pl[6:12], gla_w_gate, gla_b_gate, gla_norm, with_ctx)
    rc, rl = retention(pc[12:16], pl[12:16], rope, ret_decay, ret_norm, with_ctx)
    out_l = jnp.concatenate([dl, ml, gl, rl], axis=-1) @ w_out
    out_c = (jnp.concatenate([dc, mc, gc, rc], axis=-1) @ w_out) if with_ctx else None
    return out_c, out_l


def swiglu(h, w_in, w_out):
    gate, up = jnp.split(h @ w_in, 2, axis=-1)
    return (jax.nn.silu(gate) * up) @ w_out


def setup_inputs(seed: int = 0) -> dict:
    key = jax.random.key(seed)
    ks = jax.random.split(key, 22)

    def nrm(k, shape, s):
        return jax.random.normal(k, shape, jnp.float32) * s

    eps = 2.0 ** (-5.0 - np.arange(N_H))
    ret_logit = jnp.asarray(np.log((1.0 - eps) / eps).astype(np.float32))
    return {
        'x': nrm(ks[0], (BATCH, SEQ, D_MODEL), 1.0),
        'c': nrm(ks[1], (BATCH, D_MODEL), 1.0),
        'ctx': nrm(ks[2], (BATCH, CTX_LEN, D_MODEL), 1.0),
        'c_ctx': nrm(ks[3], (D_MODEL,), 1.0),
        'ada_w': nrm(ks[4], (DEPTH, D_MODEL, 6 * D_MODEL), 0.5 * D_MODEL ** -0.5),
        'ada_b': nrm(ks[5], (DEPTH, 6 * D_MODEL), 0.02),
        'norm_g': 1.0 + nrm(ks[6], (DEPTH, 4, D_MODEL), 0.02),
        'w_in': nrm(ks[7], (DEPTH, D_MODEL, IN_WIDTH), D_MODEL ** -0.5),
        'diff_lam': nrm(ks[8], (DEPTH, 4, DIFF_QK), 0.1),
        'diff_norm': 1.0 + nrm(ks[9], (DEPTH, HEAD_DIM), 0.02),
        'mla_q_norm': 1.0 + nrm(ks[10], (DEPTH, MLA_Q_RANK), 0.02),
        'mla_kv_norm': 1.0 + nrm(ks[11], (DEPTH, MLA_KV_RANK), 0.02),
        'mla_w_uq': nrm(ks[12], (DEPTH, MLA_Q_RANK, N_H * (MLA_NOPE + MLA_ROPE)), MLA_Q_RANK ** -0.5),
        'mla_w_ukv': nrm(ks[13], (DEPTH, MLA_KV_RANK, N_H * (MLA_NOPE + HEAD_DIM)), MLA_KV_RANK ** -0.5),
        'gla_w_gate': nrm(ks[14], (DEPTH, 2, GLA_GATE_RANK, N_H * GLA_DK), GLA_GATE_RANK ** -0.5),
        'gla_b_gate': nrm(ks[15], (DEPTH, 2, N_H * GLA_DK), 0.1),
        'gla_norm': 1.0 + nrm(ks[16], (DEPTH, HEAD_DIM), 0.02),
        'ret_decay': ret_logit + nrm(ks[17], (DEPTH, 2, N_H), 0.01),
        'ret_norm': 1.0 + nrm(ks[18], (DEPTH, HEAD_DIM), 0.02),
        'w_out': nrm(ks[19], (DEPTH, MIX_W, D_MODEL), MIX_W ** -0.5),
        'ffn_w_in': nrm(ks[20], (DEPTH, D_MODEL, 2 * FFN_HIDDEN), D_MODEL ** -0.5),
        'ffn_w_out': nrm(ks[21], (DEPTH, FFN_HIDDEN, D_MODEL), FFN_HIDDEN ** -0.5),
    }


def reference(x, c, ctx, c_ctx, ada_w, ada_b, norm_g, w_in, diff_lam, diff_norm,
              mla_q_norm, mla_kv_norm, mla_w_uq, mla_w_ukv, gla_w_gate, gla_b_gate,
              gla_norm, ret_decay, ret_norm, w_out, ffn_w_in, ffn_w_out):
    rows = x.shape[1] // GRID_W
    rope = axial_rope(rows, ROT_DIM)
    s_lat = jax.nn.silu(c)
    s_ctx = jax.nn.silu(c_ctx)
    xl, xc = x, ctx
    for l in range(DEPTH):
        with_ctx = l < DEPTH - 1
        lambda_init = 0.8 - 0.6 * math.exp(-0.3 * l)
        mod_l = jnp.split((s_lat @ ada_w[l] + ada_b[l])[:, None, :], 6, axis=-1)
        mod_c = jnp.split(s_ctx @ ada_w[l] + ada_b[l], 6, axis=-1)
        g = norm_g[l]
        hl = modulate(rms_norm(xl, g[0]), mod_l[0], mod_l[1])
        hc = modulate(rms_norm(xc, g[0]), mod_c[0], mod_c[1])
        mc, ml = hybrid_mixer(hc, hl, rope, lambda_init, with_ctx, w_in[l], diff_lam[l], diff_norm[l],
                              mla_q_norm[l], mla_kv_norm[l], mla_w_uq[l], mla_w_ukv[l],
                              gla_w_gate[l], gla_b_gate[l], gla_norm[l], ret_decay[l], ret_norm[l], w_out[l])
        xl = xl + mod_l[2] * rms_norm(ml, g[1])
        hl = modulate(rms_norm(xl, g[2]), mod_l[3], mod_l[4])
        xl = xl + mod_l[5] * rms_norm(swiglu(hl, ffn_w_in[l], ffn_w_out[l]), g[3])
        if with_ctx:
            xc = xc + mod_c[2] * rms_norm(mc, g[1])
            hc = modulate(rms_norm(xc, g[2]), mod_c[3], mod_c[4])
            xc = xc + mod_c[5] * rms_norm(swiglu(hc, ffn_w_in[l], ffn_w_out[l]), g[3])
    return xl
```

```python
import functools
import math

import jax
import jax.numpy as jnp
import numpy as np
from jax import lax
from jax.experimental import pallas as pl
from jax.experimental.pallas import tpu as pltpu

GRID_W = 64
N_H = 4
HEAD_DIM = 64
GROUP_W = N_H * HEAD_DIM
ROT_DIM = 32
ROPE_THETA = 10000.0
DIFF_QK = 32
MLA_NOPE = 64
MLA_ROPE = 32
MLA_SLOT = 128
GLA_DK = 32
GLA_GATE_RANK = 16
GLA_GATE_NORM = 16.0
GLA_CHUNK = 64
RET_DK = 32
RET_CHUNK = 256
RMS_EPS = 1e-6
NEG_BIG = -1e30

VMEM_LIMIT_BYTES = 48 * 1024 * 1024
F32 = jnp.float32
BF16 = jnp.bfloat16
HIGHEST = lax.Precision.HIGHEST

C_DQ, C_DK, C_DV, C_CQ, C_CKV, C_MISC = 0, 256, 512, 768, 1024, 1152
C_GQ, C_GK, C_GV, C_GR, C_RQ, C_RK, C_RV, C_RG, C_END = 1280, 1408, 1536, 1792, 2048, 2176, 2304, 2560, 2816
MISC_KR = 64


def _dot(a, b, precision=None):
    return jnp.dot(a, b, preferred_element_type=F32, precision=precision)


def _dot_nt(a, b):
    return lax.dot_general(a, b, (((1,), (1,)), ((), ())), preferred_element_type=F32)


def _dot_tn(a, b):
    return lax.dot_general(a, b, (((0,), (0,)), ((), ())), preferred_element_type=F32)


def _rms(x, g):
    ms = jnp.mean(x * x, axis=-1, keepdims=True)
    return x * lax.rsqrt(ms + RMS_EPS) * g


def _silu(x):
    return x * (1.0 / (1.0 + jnp.exp(-x)))


def _log_sigmoid(x):
    return jnp.minimum(x, 0.0) - jnp.log1p(jnp.exp(-jnp.abs(x)))


def _iota(shape, dim):
    return lax.broadcasted_iota(jnp.int32, shape, dim)


def _group_mean_matrix(width, group):
    same = (_iota((width, width), 0) // group) == (_iota((width, width), 1) // group)
    return jnp.where(same, 1.0 / group, 0.0).astype(BF16)


def _group_norm(o, g, gmat):
    ms = _dot((o * o).astype(BF16), gmat)
    return o * lax.rsqrt(ms + RMS_EPS) * g


def _rope(t, cs, sn):
    w = t.shape[-1]
    lo = (_iota((1, w), 1) % ROT_DIM) < (ROT_DIM // 2)
    swapped = jnp.where(lo, pltpu.roll(t, w - ROT_DIM // 2, 1), pltpu.roll(t, ROT_DIM // 2, 1))
    return t * cs + swapped * sn


def _const_spec(shape):
    nd = len(shape)
    return pl.BlockSpec(shape, lambda *_: (0,) * nd)


def _params(*sem):
    return pltpu.CompilerParams(dimension_semantics=sem, vmem_limit_bytes=VMEM_LIMIT_BYTES)


def _ada_kernel(s_ref, w_ref, b_ref, o_ref):
    s = _silu(s_ref[...])
    o_ref[0] = _dot(s, w_ref[0], precision=HIGHEST) + b_ref[0]


def _ada_call(s_all, ada_w, ada_b):
    depth, d, n = ada_w.shape
    rows = s_all.shape[0]
    tn = 1024
    return pl.pallas_call(
        _ada_kernel,
        grid=(depth, n // tn),
        in_specs=[pl.BlockSpec((rows, d), lambda l, j: (0, 0)),
                  pl.BlockSpec((1, d, tn), lambda l, j: (l, 0, j)),
                  pl.BlockSpec((1, 1, tn), lambda l, j: (l, 0, j))],
        out_specs=pl.BlockSpec((1, rows, tn), lambda l, j: (l, 0, j)),
        out_shape=jax.ShapeDtypeStruct((depth, rows, n), F32),
        compiler_params=_params("parallel", "parallel"),
        name="ada_mod",
    )(s_all, ada_w, ada_b.reshape(depth, 1, n))


def _proj_kernel(*refs, rotate):
    (x_ref, mod_ref, ng_ref, wm_ref, qg_ref, kvg_ref, wuq_ref, wuk_ref, wuv_ref,
     wgf_ref, wgb_ref, bg_ref) = refs[:12]
    refs = refs[12:]
    if rotate:
        cs256_ref, sn256_ref, cs512_ref, sn512_ref = refs[:4]
        refs = refs[4:]
    (dq_ref, dk_ref, dv_ref, mq_ref, mk_ref, mv_ref, gq_ref, gk_ref, gv_ref, gr_ref,
     gaf_ref, gab_ref, rq_ref, rk_ref, rv_ref, rg_ref) = refs

    x = x_ref[0]
    h = (_rms(x, ng_ref[0:1]) * (1.0 + mod_ref[0, 1:2]) + mod_ref[0, 0:1]).astype(BF16)

    def proj(lo, hi):
        return _dot(h, wm_ref[:, lo:hi])

    def rope256(t):
        return _rope(t, cs256_ref[...], sn256_ref[...]) if rotate else t

    def rope128(t):
        return _rope(t, cs256_ref[:, :128], sn256_ref[:, :128]) if rotate else t

    def rope512(t):
        return _rope(t, cs512_ref[...], sn512_ref[...]) if rotate else t

    dq_ref[0] = (rope256(proj(C_DQ, C_DK)) * DIFF_QK ** -0.5).astype(BF16)
    dk_ref[0] = rope256(proj(C_DK, C_DV)).astype(BF16)
    dv_ref[0] = proj(C_DV, C_CQ).astype(BF16)

    qn = _rms(proj(C_CQ, C_CKV), qg_ref[...]).astype(BF16)
    q512 = (rope512(_dot(qn, wuq_ref[...])) * (MLA_NOPE + MLA_ROPE) ** -0.5).astype(BF16)
    kvn = _rms(proj(C_CKV, C_MISC), kvg_ref[...]).astype(BF16)
    misc = proj(C_MISC, C_GQ)
    lane = _iota((1, 128), 1)
    kr = jnp.where((lane >= MISC_KR) & (lane < MISC_KR + MLA_ROPE), misc, 0.0)
    k512 = rope512(_dot(kvn, wuk_ref[...]) + jnp.concatenate([kr] * N_H, axis=1)).astype(BF16)
    for hd in range(N_H):
        mq_ref[0, hd] = q512[:, hd * MLA_SLOT:(hd + 1) * MLA_SLOT]
        mk_ref[0, hd] = k512[:, hd * MLA_SLOT:(hd + 1) * MLA_SLOT]
    mv_ref[0] = _dot(kvn, wuv_ref[...]).astype(BF16)

    gq_ref[0] = proj(C_GQ, C_GK) * GLA_DK ** -0.5
    gk_ref[0] = proj(C_GK, C_GV)
    gv_ref[0] = proj(C_GV, C_GR)
    gr_ref[0] = proj(C_GR, C_RQ)
    gaf_ref[0] = _log_sigmoid(_dot(misc, wgf_ref[...], precision=HIGHEST) + bg_ref[0:1]) / GLA_GATE_NORM
    gab_ref[0] = _log_sigmoid(_dot(misc, wgb_ref[...], precision=HIGHEST) + bg_ref[1:2]) / GLA_GATE_NORM

    rq_ref[0] = rope128(proj(C_RQ, C_RK))
    rk_ref[0] = rope128(proj(C_RK, C_RV) * RET_DK ** -0.5)
    rv_ref[0] = proj(C_RV, C_RG)
    rg_ref[0] = proj(C_RG, C_END)


_HEAD_MAJOR = ("mq", "mk")
_PROJ_OUT = (("dq", 256, BF16), ("dk", 256, BF16), ("dv", 256, BF16),
             ("mq", MLA_SLOT, BF16), ("mk", MLA_SLOT, BF16), ("mv", 256, BF16),
             ("gq", 128, F32), ("gk", 128, F32), ("gv", 256, F32), ("gr", 256, F32),
             ("gaf", 128, F32), ("gab", 128, F32),
             ("rq", 128, F32), ("rk", 128, F32), ("rv", 256, F32), ("rg", 256, F32))


def _proj_call(x, mod, ng, lw, rope_tabs, tm):
    b, n, d = x.shape
    rotate = rope_tabs is not None
    per_batch_mod = mod.shape[0] > 1
    weights = (lw["wm"], lw["qg"], lw["kvg"], lw["wuq"], lw["wuk"], lw["wuv"], lw["wgf"], lw["wgb"], lw["bg"])
    in_specs = [pl.BlockSpec((1, tm, d), lambda i, j: (i, j, 0)),
                pl.BlockSpec((1, 6, d), (lambda i, j: (i, 0, 0)) if per_batch_mod else (lambda i, j: (0, 0, 0))),
                _const_spec(ng.shape)]
    in_specs += [_const_spec(w.shape) for w in weights]
    args = [x, mod, ng, *weights]
    if rotate:
        in_specs += [pl.BlockSpec((tm, t.shape[1]), lambda i, j: (j, 0)) for t in rope_tabs]
        args += list(rope_tabs)
    out_specs, out_shape = [], []
    for name, w, dt in _PROJ_OUT:
        if name in _HEAD_MAJOR:
            out_specs.append(pl.BlockSpec((1, N_H, tm, w), lambda i, j: (i, 0, j, 0)))
            out_shape.append(jax.ShapeDtypeStruct((b, N_H, n, w), dt))
        else:
            out_specs.append(pl.BlockSpec((1, tm, w), lambda i, j: (i, j, 0)))
            out_shape.append(jax.ShapeDtypeStruct((b, n, w), dt))
    outs = pl.pallas_call(
        functools.partial(_proj_kernel, rotate=rotate),
        grid=(b, n // tm),
        in_specs=in_specs,
        out_specs=out_specs,
        out_shape=out_shape,
        compiler_params=_params("parallel", "parallel"),
        name="in_proj_rot" if rotate else "in_proj",
    )(*args)
    return {name: o for (name, _, _), o in zip(_PROJ_OUT, outs)}


def _attn_kernel(*refs, diff, has_lat, tk, lambda_init):
    q_ref, kc_ref, vc_ref = refs[:3]
    refs = refs[3:]
    if has_lat:
        kl_ref, vl_ref = refs[:2]
        refs = refs[2:]
    if diff:
        lam_ref, g_ref = refs[:2]
        refs = refs[2:]
    o_ref, acc_ref = refs

    tq = q_ref.shape[-2]
    n_maps = 2 if diff else 1
    lane_q = _iota((1, q_ref.shape[-1]), 1)
    lane_o = _iota((1, GROUP_W), 1)
    if diff:
        lv = lam_ref[...]
        lam = (jnp.exp(jnp.sum(lv[0:1] * lv[1:2], axis=-1, keepdims=True))
               - jnp.exp(jnp.sum(lv[2:3] * lv[3:4], axis=-1, keepdims=True)) + lambda_init)

    out = jnp.zeros((tq, GROUP_W), F32)
    for h in range(N_H):
        if diff:
            q = q_ref[0]
            qs = [jnp.where((lane_q // DIFF_QK) == 2 * h + m, q, jnp.zeros_like(q)) for m in range(2)]
            kc, k_of = kc_ref[0], (lambda rows: kl_ref[0, rows, :])
        else:
            qs = [q_ref[0, h]]
            kc, k_of = kc_ref[0, h], (lambda rows, h=h: kl_ref[0, h, rows, :])

        stats = []
        for m in range(n_maps):
            s = _dot_nt(qs[m], kc)
            mx = jnp.max(s, axis=-1, keepdims=True)
            p = jnp.exp(s - mx)
            stats += [mx, jnp.sum(p, axis=-1, keepdims=True)]
            acc_ref[m] = _dot(p.astype(BF16), vc_ref[0])

        if has_lat:
            def body(c, carry):
                rows = pl.ds(pl.multiple_of(c * tk, tk), tk)
                kb = k_of(rows)
                vb = vl_ref[0, rows, :]
                new = []
                for m in range(n_maps):
                    mx, l = carry[2 * m], carry[2 * m + 1]
                    s = _dot_nt(qs[m], kb)
                    mn = jnp.maximum(mx, jnp.max(s, axis=-1, keepdims=True))
                    alpha = jnp.exp(mx - mn)
                    p = jnp.exp(s - mn)
                    new += [mn, alpha * l + jnp.sum(p, axis=-1, keepdims=True)]
                    acc_ref[m] = alpha * acc_ref[m] + _dot(p.astype(BF16), vb)
                return tuple(new)

            stats = lax.fori_loop(0, kl_ref.shape[-2] // tk, body, tuple(stats))

        o_h = acc_ref[0] / stats[1]
        if diff:
            o_h = o_h - lam * (acc_ref[1] / stats[3])
        out = jnp.where((lane_o // HEAD_DIM) == h, o_h, out)

    if diff:
        out = _group_norm(out, g_ref[...], _group_mean_matrix(GROUP_W, HEAD_DIM)) * (1.0 - lambda_init)
    o_ref[0] = out.astype(o_ref.dtype)


def _attn_call(q, kc, vc, kl, vl, diff_args, tq, tk, name):
    has_lat = kl is not None
    diff = diff_args is not None
    b, lq = q.shape[0], q.shape[-2]

    def qk_spec(arr, rows, tiled):
        if arr.ndim == 4:
            return pl.BlockSpec((1, N_H, rows, arr.shape[3]), (lambda i, j: (i, 0, j, 0)) if tiled else (lambda i, j: (i, 0, 0, 0)))
        return pl.BlockSpec((1, rows, arr.shape[2]), (lambda i, j: (i, j, 0)) if tiled else (lambda i, j: (i, 0, 0)))

    in_specs = [qk_spec(q, tq, True), qk_spec(kc, kc.shape[-2], False), qk_spec(vc, vc.shape[1], False)]
    args = [q, kc, vc]
    if has_lat:
        in_specs += [qk_spec(kl, kl.shape[-2], False), qk_spec(vl, vl.shape[1], False)]
        args += [kl, vl]
    lambda_init = 0.0
    if diff:
        lam_vecs, sub_g, lambda_init = diff_args
        in_specs += [_const_spec(lam_vecs.shape), _const_spec(sub_g.shape)]
        args += [lam_vecs, sub_g]
    return pl.pallas_call(
        functools.partial(_attn_kernel, diff=diff, has_lat=has_lat, tk=tk, lambda_init=lambda_init),
        grid=(b, lq // tq),
        in_specs=in_specs,
        out_specs=pl.BlockSpec((1, tq, GROUP_W), lambda i, j: (i, j, 0)),
        out_shape=jax.ShapeDtypeStruct((b, lq, GROUP_W), BF16),
        scratch_shapes=[pltpu.VMEM((2 if diff else 1, tq, GROUP_W), F32)],
        compiler_params=_params("parallel", "parallel"),
        name=name,
    )(*args)


def _gla_chunk(q, k, v, a, st_ref, rev, consts):
    tri_f, tri_b, rexp, bdmask, rowid = consts
    c = GLA_CHUNK
    b = _dot(tri_b if rev else tri_f, a, precision=HIGHEST)
    qk_tiles = c // 8

    pieces, spans = [], []
    for j in range(c):
        t = j // 8
        t0, t1 = (0, t + 1) if rev else (t, qk_tiles)
        r0, r1 = 8 * t0, 8 * t1
        rid = rowid[r0:r1]
        keep = (rid <= j) if rev else (rid >= j)
        arg = jnp.where(keep, b[r0:r1] - b[j:j + 1], NEG_BIG)
        pieces.append(jnp.exp(arg) * (q[r0:r1] * k[j:j + 1]))
        spans.append((t0, t1))
    e_all = jnp.concatenate(pieces, axis=0).astype(BF16)
    sx = _dot(e_all, rexp)
    acc = [jnp.zeros((8, GROUP_W), F32) for _ in range(qk_tiles)]
    off = 0
    for j in range(c):
        t0, t1 = spans[j]
        vj = v[j:j + 1]
        for t in range(t0, t1):
            acc[t] = acc[t] + sx[off:off + 8] * vj
            off += 8
    o = jnp.concatenate(acc, axis=0)

    st = st_ref[...]
    o = o + _dot_nt((q * jnp.exp(b)).astype(BF16), st.astype(BF16))
    b_end = b[0:1] if rev else b[c - 1:c]
    ks = (k * jnp.exp(b_end - b)).astype(BF16)
    upd = _dot_tn(v.astype(BF16), ks)
    st_ref[...] = st * jnp.exp(b_end) + jnp.where(bdmask, upd, 0.0)
    return o


def _gla_kernel(qc_ref, kc_ref, vc_ref, afc_ref, abc_ref, ql_ref, kl_ref, vl_ref, afl_ref, abl_ref,
                oc_ref, ol_ref, sf_ref, sb_ref):
    c = GLA_CHUNK
    ri, ci = _iota((c, c), 0), _iota((c, c), 1)
    tri_f = jnp.where(ri >= ci, 1.0, 0.0).astype(F32)
    tri_b = jnp.where(ri <= ci, 1.0, 0.0).astype(F32)
    rexp = jnp.where((_iota((128, GROUP_W), 0) // GLA_DK) == (_iota((128, GROUP_W), 1) // HEAD_DIM),
                     1.0, 0.0).astype(BF16)
    bdmask = (_iota((GROUP_W, 128), 0) // HEAD_DIM) == (_iota((GROUP_W, 128), 1) // GLA_DK)
    rowid = _iota((c, 1), 0)
    consts = (tri_f, tri_b, rexp, bdmask, rowid)

    sf_ref[...] = jnp.zeros_like(sf_ref)
    sb_ref[...] = jnp.zeros_like(sb_ref)
    oc_ref[...] = jnp.zeros_like(oc_ref)
    ol_ref[...] = jnp.zeros_like(ol_ref)

    def sweep(q_ref, k_ref, v_ref, af_ref, ab_ref, o_ref):
        n = q_ref.shape[1] // c

        def body(i, carry):
            for rev, idx, a_ref, st_ref in ((False, i, af_ref, sf_ref), (True, n - 1 - i, ab_ref, sb_ref)):
                rows = pl.ds(pl.multiple_of(idx * c, c), c)
                o = _gla_chunk(q_ref[0, rows, :], k_ref[0, rows, :], v_ref[0, rows, :], a_ref[0, rows, :],
                               st_ref, rev, consts)
                o_ref[0, rows, :] += o
            return carry

        lax.fori_loop(0, n, body, 0)

    sweep(qc_ref, kc_ref, vc_ref, afc_ref, abc_ref, oc_ref)
    sweep(ql_ref, kl_ref, vl_ref, afl_ref, abl_ref, ol_ref)


def _seq_spec(arr):
    return pl.BlockSpec((1,) + arr.shape[1:], lambda i: (i, 0, 0))


def _gla_call(pc, plat):
    b = pc["gq"].shape[0]
    args = [pc["gq"], pc["gk"], pc["gv"], pc["gaf"], pc["gab"],
            plat["gq"], plat["gk"], plat["gv"], plat["gaf"], plat["gab"]]
    lc, ll = pc["gq"].shape[1], plat["gq"].shape[1]
    return pl.pallas_call(
        _gla_kernel,
        grid=(b,),
        in_specs=[_seq_spec(a) for a in args],
        out_specs=[pl.BlockSpec((1, lc, GROUP_W), lambda i: (i, 0, 0)),
                   pl.BlockSpec((1, ll, GROUP_W), lambda i: (i, 0, 0))],
        out_shape=[jax.ShapeDtypeStruct((b, lc, GROUP_W), F32), jax.ShapeDtypeStruct((b, ll, GROUP_W), F32)],
        scratch_shapes=[pltpu.VMEM((GROUP_W, 128), F32), pltpu.VMEM((GROUP_W, 128), F32)],
        compiler_params=_params("parallel"),
        name="gla_scan",
    )(*args)


def _ret_kernel(lg_ref, qc_ref, kc_ref, vc_ref, ql_ref, kl_ref, vl_ref, oc_ref, ol_ref,
                sf_ref, sb_ref, dm_ref):
    c = RET_CHUNK
    lgam = _log_sigmoid(lg_ref[...])
    ri, ci = _iota((c, c), 0), _iota((c, c), 1)
    pos = _iota((c, 128), 0).astype(F32)
    dq, dk, dc = [], [], []
    for d in range(2):
        lg = lgam[d:d + 1]
        dist = (ri - ci) if d == 0 else (ci - ri)
        for h in range(N_H):
            lg_h = lg[:, h * RET_DK:h * RET_DK + 1]
            dm_ref[d, h] = jnp.exp(jnp.where(dist >= 0, dist.astype(F32) * lg_h, NEG_BIG))
        dq.append(jnp.exp(((pos + 1.0) if d == 0 else (c - pos)) * lg))
        dk.append(jnp.exp(((c - 1.0 - pos) if d == 0 else pos) * lg))
        dc.append(jnp.exp(c * lg))
    bdmask = (_iota((GROUP_W, 128), 0) // HEAD_DIM) == (_iota((GROUP_W, 128), 1) // RET_DK)
    lane_q = _iota((1, 128), 1)
    lane_o = _iota((1, GROUP_W), 1)

    sf_ref[...] = jnp.zeros_like(sf_ref)
    sb_ref[...] = jnp.zeros_like(sb_ref)
    oc_ref[...] = jnp.zeros_like(oc_ref)
    ol_ref[...] = jnp.zeros_like(ol_ref)

    def chunk(q, k, v, st_ref, d):
        st = st_ref[...]
        k16, v16 = k.astype(BF16), v.astype(BF16)
        o = _dot_nt((q * dq[d]).astype(BF16), st.astype(BF16))
        for h in range(N_H):
            qh = jnp.where((lane_q // RET_DK) == h, q, 0.0).astype(BF16)
            s = _dot_nt(qh, k16) * dm_ref[d, h]
            o = o + jnp.where((lane_o // HEAD_DIM) == h, _dot(s.astype(BF16), v16), 0.0)
        upd = _dot_tn(v16, (k * dk[d]).astype(BF16))
        st_ref[...] = st * dc[d] + jnp.where(bdmask, upd, 0.0)
        return o

    def sweep(q_ref, k_ref, v_ref, o_ref):
        n = q_ref.shape[1] // c

        def body(i, carry):
            for d, idx, st_ref in ((0, i, sf_ref), (1, n - 1 - i, sb_ref)):
                rows = pl.ds(pl.multiple_of(idx * c, c), c)
                o_ref[0, rows, :] += chunk(q_ref[0, rows, :], k_ref[0, rows, :], v_ref[0, rows, :], st_ref, d)
            return carry

        lax.fori_loop(0, n, body, 0)

    sweep(qc_ref, kc_ref, vc_ref, oc_ref)
    sweep(ql_ref, kl_ref, vl_ref, ol_ref)


def _ret_call(lg, pc, plat):
    b = pc["rq"].shape[0]
    args = [pc["rq"], pc["rk"], pc["rv"], plat["rq"], plat["rk"], plat["rv"]]
    lc, ll = pc["rq"].shape[1], plat["rq"].shape[1]
    return pl.pallas_call(
        _ret_kernel,
        grid=(b,),
        in_specs=[_const_spec(lg.shape)] + [_seq_spec(a) for a in args],
        out_specs=[pl.BlockSpec((1, lc, GROUP_W), lambda i: (i, 0, 0)),
                   pl.BlockSpec((1, ll, GROUP_W), lambda i: (i, 0, 0))],
        out_shape=[jax.ShapeDtypeStruct((b, lc, GROUP_W), F32), jax.ShapeDtypeStruct((b, ll, GROUP_W), F32)],
        scratch_shapes=[pltpu.VMEM((GROUP_W, 128), F32), pltpu.VMEM((GROUP_W, 128), F32),
                        pltpu.VMEM((2, N_H, RET_CHUNK, RET_CHUNK), F32)],
        compiler_params=_params("parallel"),
        name="ret_scan",
    )(lg, *args)


def _mix_out_kernel(x_ref, mod_ref, ng_ref, do_ref, mo_ref, go_ref, gr_ref, ro_ref, rg_ref,
                    gg_ref, rgn_ref, wo_ref, x1_ref, h2_ref):
    gmat = _group_mean_matrix(GROUP_W, HEAD_DIM)
    gl = _group_norm(go_ref[0], gg_ref[...], gmat) * _silu(gr_ref[0])
    rt = _group_norm(ro_ref[0], rgn_ref[...], gmat) * _silu(rg_ref[0])
    ml = (_dot(do_ref[0], wo_ref[0:256]) + _dot(mo_ref[0], wo_ref[256:512])
          + _dot(gl.astype(BF16), wo_ref[512:768]) + _dot(rt.astype(BF16), wo_ref[768:1024]))
    x1 = x_ref[0] + mod_ref[0, 2:3] * _rms(ml, ng_ref[1:2])
    x1_ref[0] = x1
    h2_ref[0] = (_rms(x1, ng_ref[2:3]) * (1.0 + mod_ref[0, 4:5]) + mod_ref[0, 3:4]).astype(BF16)


def _mod_spec(mod, d):
    if mod.shape[0] > 1:
        return pl.BlockSpec((1, 6, d), lambda i, j: (i, 0, 0))
    return pl.BlockSpec((1, 6, d), lambda i, j: (0, 0, 0))


def _mix_out_call(x, mod, ng, d_o, m_o, g_o, g_r, r_o, r_g, lw, tm):
    b, n, d = x.shape

    def tok(w):
        return pl.BlockSpec((1, tm, w), lambda i, j: (i, j, 0))

    return pl.pallas_call(
        _mix_out_kernel,
        grid=(b, n // tm),
        in_specs=[tok(d), _mod_spec(mod, d), _const_spec(ng.shape)] + [tok(GROUP_W)] * 6
                 + [_const_spec(lw["gla_g"].shape), _const_spec(lw["ret_g"].shape), _const_spec(lw["w_out"].shape)],
        out_specs=[tok(d), tok(d)],
        out_shape=[jax.ShapeDtypeStruct((b, n, d), F32), jax.ShapeDtypeStruct((b, n, d), BF16)],
        compiler_params=_params("parallel", "parallel"),
        name="mix_out",
    )(x, mod, ng, d_o, m_o, g_o, g_r, r_o, r_g, lw["gla_g"], lw["ret_g"], lw["w_out"])


def _ffn_kernel(h_ref, x1_ref, mod_ref, ng_ref, wg_ref, wu_ref, wo_ref, x2_ref):
    h = h_ref[0]
    a = (_silu(_dot(h, wg_ref[...])) * _dot(h, wu_ref[...])).astype(BF16)
    y = _dot(a, wo_ref[...])
    x2_ref[0] = x1_ref[0] + mod_ref[0, 5:6] * _rms(y, ng_ref[3:4])


def _ffn_call(h2, x1, mod, ng, lw, tm):
    b, n, d = x1.shape

    def tok(w):
        return pl.BlockSpec((1, tm, w), lambda i, j: (i, j, 0))

    return pl.pallas_call(
        _ffn_kernel,
        grid=(b, n // tm),
        in_specs=[tok(d), tok(d), _mod_spec(mod, d), _const_spec(ng.shape),
                  _const_spec(lw["ffn_g"].shape), _const_spec(lw["ffn_u"].shape), _const_spec(lw["ffn_o"].shape)],
        out_specs=tok(d),
        out_shape=jax.ShapeDtypeStruct((b, n, d), F32),
        compiler_params=_params("parallel", "parallel"),
        name="ffn",
    )(h2, x1, mod, ng, lw["ffn_g"], lw["ffn_u"], lw["ffn_o"])


def _pack_layer(l, w_in, diff_norm, mla_q_norm, mla_kv_norm, mla_w_uq, mla_w_ukv, gla_w_gate, gla_b_gate,
                gla_norm, ret_decay, ret_norm, w_out, ffn_w_in, ffn_w_out):
    d = w_in.shape[1]
    w = w_in[l]
    sizes = (256, 256, 256, 256, 128, 32, 128, 128, 256, 256, 16, 16, 128, 128, 256, 256)
    offs = np.concatenate([[0], np.cumsum(sizes)])
    seg = [w[:, offs[i]:offs[i + 1]] for i in range(len(sizes))]
    z = lambda n: jnp.zeros((d, n), w.dtype)
    misc = jnp.concatenate([seg[10], seg[11], z(32), seg[5], z(32)], axis=1)
    wm = jnp.concatenate(seg[0:5] + [misc] + seg[6:10] + seg[12:16], axis=1).astype(BF16)

    uq = mla_w_uq[l].reshape(-1, N_H, MLA_NOPE + MLA_ROPE)
    uq = jnp.pad(uq, ((0, 0), (0, 0), (0, MLA_SLOT - MLA_NOPE - MLA_ROPE))).reshape(-1, N_H * MLA_SLOT)
    ukv = mla_w_ukv[l].reshape(-1, N_H, MLA_NOPE + HEAD_DIM)
    uk = jnp.pad(ukv[:, :, :MLA_NOPE], ((0, 0), (0, 0), (0, MLA_SLOT - MLA_NOPE))).reshape(-1, N_H * MLA_SLOT)
    uv = ukv[:, :, MLA_NOPE:].reshape(-1, GROUP_W)

    wgf = jnp.zeros((128, 128), F32).at[0:GLA_GATE_RANK].set(gla_w_gate[l, 0])
    wgb = jnp.zeros((128, 128), F32).at[GLA_GATE_RANK:2 * GLA_GATE_RANK].set(gla_w_gate[l, 1])
    hid = ffn_w_out.shape[1]
    return {
        "wm": wm, "qg": mla_q_norm[l][None], "kvg": mla_kv_norm[l][None],
        "wuq": uq.astype(BF16), "wuk": uk.astype(BF16), "wuv": uv.astype(BF16),
        "wgf": wgf, "wgb": wgb, "bg": gla_b_gate[l],
        "diff_g": jnp.tile(diff_norm[l], N_H)[None], "gla_g": jnp.tile(gla_norm[l], N_H)[None],
        "ret_g": jnp.tile(ret_norm[l], N_H)[None],
        "ret_lg": jnp.repeat(ret_decay[l], RET_DK, axis=1),
        "w_out": w_out[l].astype(BF16),
        "ffn_g": ffn_w_in[l][:, :hid].astype(BF16), "ffn_u": ffn_w_in[l][:, hid:].astype(BF16),
        "ffn_o": ffn_w_out[l].astype(BF16),
    }


def _rope_tables(seq):
    rows = seq // GRID_W
    row = jnp.repeat(jnp.arange(rows, dtype=F32), GRID_W)
    col = jnp.tile(jnp.arange(GRID_W, dtype=F32), rows)
    n_freq = ROT_DIM // 4
    freqs = ROPE_THETA ** (-jnp.arange(n_freq, dtype=F32) / n_freq)
    ang = jnp.concatenate([row[:, None] * freqs, col[:, None] * freqs], axis=-1)
    cos, sin = jnp.cos(ang), jnp.sin(ang)
    c32 = jnp.concatenate([cos, cos], axis=-1)
    s32 = jnp.concatenate([-sin, sin], axis=-1)
    one, zero = jnp.ones((seq, 1), F32), jnp.zeros((seq, 1), F32)
    slot_c = jnp.concatenate([jnp.tile(one, (1, MLA_NOPE)), c32, jnp.tile(one, (1, MLA_SLOT - MLA_NOPE - MLA_ROPE))], -1)
    slot_s = jnp.concatenate([jnp.tile(zero, (1, MLA_NOPE)), s32, jnp.tile(zero, (1, MLA_SLOT - MLA_NOPE - MLA_ROPE))], -1)
    return (jnp.tile(c32, (1, 256 // ROT_DIM)), jnp.tile(s32, (1, 256 // ROT_DIM)),
            jnp.tile(slot_c, (1, N_H)), jnp.tile(slot_s, (1, N_H)))


def _tile(n, want):
    t = min(n, want)
    while n % t:
        t //= 2
    return t


def kernel(x, c, ctx, c_ctx, ada_w, ada_b, norm_g, w_in, diff_lam, diff_norm, mla_q_norm, mla_kv_norm,
           mla_w_uq, mla_w_ukv, gla_w_gate, gla_b_gate, gla_norm, ret_decay, ret_norm, w_out, ffn_w_in, ffn_w_out):
    b, seq, d = x.shape
    lc = ctx.shape[1]
    depth = ada_w.shape[0]
    rope_tabs = _rope_tables(seq)

    pad = (-(b + 1)) % 8
    s_all = jnp.concatenate([c, c_ctx[None], jnp.zeros((pad, d), F32)], axis=0)
    mods = _ada_call(s_all, ada_w, ada_b)

    tm_l, tm_c = _tile(seq, 256), _tile(lc, 256)
    tq_l, tq_c = _tile(seq, 256), _tile(lc, 256)
    tk = _tile(seq, 512)

    xl, xc = x, ctx
    for l in range(depth):
        with_ctx = l < depth - 1
        lambda_init = 0.8 - 0.6 * math.exp(-0.3 * l)
        lw = _pack_layer(l, w_in, diff_norm, mla_q_norm, mla_kv_norm, mla_w_uq, mla_w_ukv, gla_w_gate,
                         gla_b_gate, gla_norm, ret_decay, ret_norm, w_out, ffn_w_in, ffn_w_out)
        mod_l = mods[l, :b].reshape(b, 6, d)
        mod_c = mods[l, b:b + 1].reshape(1, 6, d)
        ng = norm_g[l]

        pc = _proj_call(xc, mod_c, ng, lw, None, tm_c)
        pt = _proj_call(xl, mod_l, ng, lw, rope_tabs, tm_l)

        dargs = (diff_lam[l], lw["diff_g"], lambda_init)
        d_l = _attn_call(pt["dq"], pc["dk"], pc["dv"], pt["dk"], pt["dv"], dargs, tq_l, tk, "diff_attn")
        m_l = _attn_call(pt["mq"], pc["mk"], pc["mv"], pt["mk"], pt["mv"], None, tq_l, tk, "mla_attn")
        g_c, g_l = _gla_call(pc, pt)
        r_c, r_l = _ret_call(lw["ret_lg"], pc, pt)

        x1, h2 = _mix_out_call(xl, mod_l, ng, d_l, m_l, g_l, pt["gr"], r_l, pt["rg"], lw, tm_l)
        xl = _ffn_call(h2, x1, mod_l, ng, lw, tm_l)
        if with_ctx:
            d_c = _attn_call(pc["dq"], pc["dk"], pc["dv"], None, None, dargs, tq_c, tk, "diff_attn_ctx")
            m_c = _attn_call(pc["mq"], pc["mk"], pc["mv"], None, None, None, tq_c, tk, "mla_attn_ctx")
            x1c, h2c = _mix_out_call(xc, mod_c, ng, d_c, m_c, g_c, pc["gr"], r_c, pc["rg"], lw, tm_c)
            xc = _ffn_call(h2c, x1c, mod_c, ng, lw, tm_c)
    return xl
```

```python
import functools
import math

import jax
import jax.numpy as jnp
import numpy as np
from jax import lax
from jax.experimental import pallas as pl
from jax.experimental.pallas import tpu as pltpu

GRID_W = 64
N_H = 4
HEAD_DIM = 64
GROUP_W = N_H * HEAD_DIM
ROT_DIM = 32
ROPE_THETA = 10000.0
DIFF_QK = 32
MLA_NOPE = 64
MLA_ROPE = 32
MLA_SLOT = 128
V_SLOT = 128
V_ONES = HEAD_DIM
LOG2E = math.log2(math.e)
GLA_DK = 32
GLA_GATE_RANK = 16
GLA_GATE_NORM = 16.0
GLA_CHUNK = 64
RET_DK = 32
RET_CHUNK = 256
RMS_EPS = 1e-6
NEG_BIG = -1e30

VMEM_LIMIT_BYTES = 48 * 1024 * 1024
F32 = jnp.float32
BF16 = jnp.bfloat16
HIGHEST = lax.Precision.HIGHEST

C_DQ, C_DK, C_DV, C_CQ, C_CKV, C_MISC = 0, 256, 512, 768, 1024, 1152
C_GQ, C_GK, C_GV, C_GR, C_RQ, C_RK, C_RV, C_RG, C_END = 1280, 1408, 1536, 1792, 2048, 2176, 2304, 2560, 2816
MISC_KR = 64


def _dot(a, b, precision=None):
    return jnp.dot(a, b, preferred_element_type=F32, precision=precision)


def _dot_nt(a, b):
    return lax.dot_general(a, b, (((1,), (1,)), ((), ())), preferred_element_type=F32)


def _dot_tn(a, b):
    return lax.dot_general(a, b, (((0,), (0,)), ((), ())), preferred_element_type=F32)


def _rms(x, g):
    ms = jnp.mean(x * x, axis=-1, keepdims=True)
    return x * lax.rsqrt(ms + RMS_EPS) * g


def _silu(x):
    return x * (1.0 / (1.0 + jnp.exp(-x)))


def _log_sigmoid(x):
    return jnp.minimum(x, 0.0) - jnp.log1p(jnp.exp(-jnp.abs(x)))


def _iota(shape, dim):
    return lax.broadcasted_iota(jnp.int32, shape, dim)


def _group_mean_matrix(width, group):
    same = (_iota((width, width), 0) // group) == (_iota((width, width), 1) // group)
    return jnp.where(same, 1.0 / group, 0.0).astype(BF16)


def _group_norm(o, g, gmat):
    ms = _dot((o * o).astype(BF16), gmat)
    return o * lax.rsqrt(ms + RMS_EPS) * g


def _rope(t, cs, sn):
    w = t.shape[-1]
    lo = (_iota((1, w), 1) % ROT_DIM) < (ROT_DIM // 2)
    swapped = jnp.where(lo, pltpu.roll(t, w - ROT_DIM // 2, 1), pltpu.roll(t, ROT_DIM // 2, 1))
    return t * cs + swapped * sn


def _const_spec(shape):
    nd = len(shape)
    return pl.BlockSpec(shape, lambda *_: (0,) * nd)


def _params(*sem):
    return pltpu.CompilerParams(dimension_semantics=sem, vmem_limit_bytes=VMEM_LIMIT_BYTES)


def _ada_kernel(s_ref, w_ref, b_ref, o_ref):
    s = _silu(s_ref[...])
    o_ref[0] = _dot(s, w_ref[0], precision=HIGHEST) + b_ref[0]


def _ada_call(s_all, ada_w, ada_b):
    depth, d, n = ada_w.shape
    rows = s_all.shape[0]
    tn = 1024
    return pl.pallas_call(
        _ada_kernel,
        grid=(depth, n // tn),
        in_specs=[pl.BlockSpec((rows, d), lambda l, j: (0, 0)),
                  pl.BlockSpec((1, d, tn), lambda l, j: (l, 0, j)),
                  pl.BlockSpec((1, 1, tn), lambda l, j: (l, 0, j))],
        out_specs=pl.BlockSpec((1, rows, tn), lambda l, j: (l, 0, j)),
        out_shape=jax.ShapeDtypeStruct((depth, rows, n), F32),
        compiler_params=_params("parallel", "parallel"),
        name="ada_mod",
    )(s_all, ada_w, ada_b.reshape(depth, 1, n))


def _proj_kernel(*refs, rotate):
    (x_ref, mod_ref, ng_ref, wm_ref, qg_ref, kvg_ref, wuq_ref, wuk_ref, wuv_ref,
     wgf_ref, wgb_ref, bg_ref) = refs[:12]
    refs = refs[12:]
    if rotate:
        cs256_ref, sn256_ref, cs512_ref, sn512_ref = refs[:4]
        refs = refs[4:]
    (dq_ref, dk_ref, dv_ref, mq_ref, mk_ref, mv_ref, gq_ref, gk_ref, gv_ref, gr_ref,
     gaf_ref, gab_ref, rq_ref, rk_ref, rv_ref, rg_ref) = refs

    x = x_ref[0]
    h = (_rms(x, ng_ref[0:1]) * (1.0 + mod_ref[0, 1:2]) + mod_ref[0, 0:1]).astype(BF16)

    def proj(lo, hi):
        return _dot(h, wm_ref[:, lo:hi])

    def rope256(t):
        return _rope(t, cs256_ref[...], sn256_ref[...]) if rotate else t

    def rope128(t):
        return _rope(t, cs256_ref[:, :128], sn256_ref[:, :128]) if rotate else t

    def rope512(t):
        return _rope(t, cs512_ref[...], sn512_ref[...]) if rotate else t

    lane = _iota((1, 128), 1)
    ones_lane = jnp.where(lane == V_ONES, 1.0, 0.0)

    def store_values(v_ref, v):
        rolled = pltpu.roll(v, GROUP_W - HEAD_DIM, 1)
        for hd in range(N_H):
            src = v if hd % 2 == 0 else rolled
            blk = src[:, (hd // 2) * 128:(hd // 2 + 1) * 128]
            v_ref[0, hd] = jnp.where(lane < HEAD_DIM, blk, ones_lane).astype(BF16)

    dq_ref[0] = (rope256(proj(C_DQ, C_DK)) * (DIFF_QK ** -0.5 * LOG2E)).astype(BF16)
    dk_ref[0] = rope256(proj(C_DK, C_DV)).astype(BF16)
    store_values(dv_ref, proj(C_DV, C_CQ))

    qn = _rms(proj(C_CQ, C_CKV), qg_ref[...]).astype(BF16)
    q512 = (rope512(_dot(qn, wuq_ref[...])) * ((MLA_NOPE + MLA_ROPE) ** -0.5 * LOG2E)).astype(BF16)
    kvn = _rms(proj(C_CKV, C_MISC), kvg_ref[...]).astype(BF16)
    misc = proj(C_MISC, C_GQ)
    kr = jnp.where((lane >= MISC_KR) & (lane < MISC_KR + MLA_ROPE), misc, 0.0)
    k512 = rope512(_dot(kvn, wuk_ref[...]) + jnp.concatenate([kr] * N_H, axis=1)).astype(BF16)
    for hd in range(N_H):
        mq_ref[0, hd] = q512[:, hd * MLA_SLOT:(hd + 1) * MLA_SLOT]
        mk_ref[0, hd] = k512[:, hd * MLA_SLOT:(hd + 1) * MLA_SLOT]
    store_values(mv_ref, _dot(kvn, wuv_ref[...]))

    gq_ref[0] = proj(C_GQ, C_GK) * GLA_DK ** -0.5
    gk_ref[0] = proj(C_GK, C_GV)
    gv_ref[0] = proj(C_GV, C_GR)
    gr_ref[0] = proj(C_GR, C_RQ)
    gaf_ref[0] = _log_sigmoid(_dot(misc, wgf_ref[...], precision=HIGHEST) + bg_ref[0:1]) / GLA_GATE_NORM
    gab_ref[0] = _log_sigmoid(_dot(misc, wgb_ref[...], precision=HIGHEST) + bg_ref[1:2]) / GLA_GATE_NORM

    rq_ref[0] = rope128(proj(C_RQ, C_RK))
    rk_ref[0] = rope128(proj(C_RK, C_RV) * RET_DK ** -0.5)
    rv_ref[0] = proj(C_RV, C_RG)
    rg_ref[0] = proj(C_RG, C_END)


_HEAD_MAJOR = ("mq", "mk", "dv", "mv")
_PROJ_OUT = (("dq", 256, BF16), ("dk", 256, BF16), ("dv", V_SLOT, BF16),
             ("mq", MLA_SLOT, BF16), ("mk", MLA_SLOT, BF16), ("mv", V_SLOT, BF16),
             ("gq", 128, F32), ("gk", 128, F32), ("gv", 256, F32), ("gr", 256, F32),
             ("gaf", 128, F32), ("gab", 128, F32),
             ("rq", 128, F32), ("rk", 128, F32), ("rv", 256, F32), ("rg", 256, F32))


def _proj_call(x, mod, ng, lw, rope_tabs, tm):
    b, n, d = x.shape
    rotate = rope_tabs is not None
    per_batch_mod = mod.shape[0] > 1
    weights = (lw["wm"], lw["qg"], lw["kvg"], lw["wuq"], lw["wuk"], lw["wuv"], lw["wgf"], lw["wgb"], lw["bg"])
    in_specs = [pl.BlockSpec((1, tm, d), lambda i, j: (i, j, 0)),
                pl.BlockSpec((1, 6, d), (lambda i, j: (i, 0, 0)) if per_batch_mod else (lambda i, j: (0, 0, 0))),
                _const_spec(ng.shape)]
    in_specs += [_const_spec(w.shape) for w in weights]
    args = [x, mod, ng, *weights]
    if rotate:
        in_specs += [pl.BlockSpec((tm, t.shape[1]), lambda i, j: (j, 0)) for t in rope_tabs]
        args += list(rope_tabs)
    out_specs, out_shape = [], []
    for name, w, dt in _PROJ_OUT:
        if name in _HEAD_MAJOR:
            out_specs.append(pl.BlockSpec((1, N_H, tm, w), lambda i, j: (i, 0, j, 0)))
            out_shape.append(jax.ShapeDtypeStruct((b, N_H, n, w), dt))
        else:
            out_specs.append(pl.BlockSpec((1, tm, w), lambda i, j: (i, j, 0)))
            out_shape.append(jax.ShapeDtypeStruct((b, n, w), dt))
    outs = pl.pallas_call(
        functools.partial(_proj_kernel, rotate=rotate),
        grid=(b, n // tm),
        in_specs=in_specs,
        out_specs=out_specs,
        out_shape=out_shape,
        compiler_params=_params("parallel", "parallel"),
        name="in_proj_rot" if rotate else "in_proj",
    )(*args)
    return {name: o for (name, _, _), o in zip(_PROJ_OUT, outs)}


def _attn_kernel(*refs, diff, has_lat, tk, lambda_init):
    q_ref, kc_ref, vc_ref = refs[:3]
    refs = refs[3:]
    if has_lat:
        kl_ref, vl_ref = refs[:2]
        refs = refs[2:]
    if diff:
        lam_ref, g_ref = refs[:2]
        refs = refs[2:]
    o_ref, acc_ref = refs

    n_maps = 2 if diff else 1
    chains = [(h, m) for h in range(N_H) for m in range(n_maps)]
    if diff:
        q = q_ref[0]
        lane_q = _iota((1, q.shape[-1]), 1)
        qs = [jnp.where((lane_q // DIFF_QK) == 2 * h + m, q, jnp.zeros_like(q)) for h, m in chains]
    else:
        qs = [q_ref[0, h] for h, _ in chains]

    def keys(ref, rows, h):
        return ref[0, rows, :] if diff else ref[0, h, rows, :]

    def step(k_ref, v_ref, rows, ms):
        new = []
        for i, (h, _) in enumerate(chains):
            s = _dot_nt(qs[i], keys(k_ref, rows, h))
            mx = jnp.max(s, axis=-1, keepdims=True)
            if ms is None:
                p = jnp.exp2(s - mx).astype(BF16)
                acc_ref[i] = _dot(p, v_ref[0, h, rows, :])
            else:
                mx = jnp.maximum(ms[i], mx)
                p = jnp.exp2(s - mx).astype(BF16)
                acc_ref[i] = jnp.exp2(ms[i] - mx) * acc_ref[i] + _dot(p, v_ref[0, h, rows, :])
            new.append(mx)
        return tuple(new)

    ms = step(kc_ref, vc_ref, slice(None), None)
    if has_lat:
        def body(c, carry):
            return step(kl_ref, vl_ref, pl.ds(pl.multiple_of(c * tk, tk), tk), carry)

        lax.fori_loop(0, kl_ref.shape[-2] // tk, body, ms)

    def head_out(i):
        a = acc_ref[i]
        return a[:, :HEAD_DIM] * (1.0 / a[:, V_ONES:V_ONES + 1])

    if diff:
        lv = lam_ref[...]
        lam = (jnp.exp(jnp.sum(lv[0:1] * lv[1:2], axis=-1, keepdims=True))
               - jnp.exp(jnp.sum(lv[2:3] * lv[3:4], axis=-1, keepdims=True)) + lambda_init)
        out = jnp.concatenate([head_out(2 * h) - lam * head_out(2 * h + 1) for h in range(N_H)], axis=1)
        out = _group_norm(out, g_ref[...], _group_mean_matrix(GROUP_W, HEAD_DIM)) * (1.0 - lambda_init)
    else:
        out = jnp.concatenate([head_out(h) for h in range(N_H)], axis=1)
    o_ref[0] = out.astype(o_ref.dtype)


def _attn_call(q, kc, vc, kl, vl, diff_args, tq, tk, name):
    has_lat = kl is not None
    diff = diff_args is not None
    b, lq = q.shape[0], q.shape[-2]

    def spec(arr, rows, tiled):
        if arr.ndim == 4:
            return pl.BlockSpec((1, N_H, rows, arr.shape[3]), (lambda i, j: (i, 0, j, 0)) if tiled else (lambda i, j: (i, 0, 0, 0)))
        return pl.BlockSpec((1, rows, arr.shape[2]), (lambda i, j: (i, j, 0)) if tiled else (lambda i, j: (i, 0, 0)))

    in_specs = [spec(q, tq, True), spec(kc, kc.shape[-2], False), spec(vc, vc.shape[-2], False)]
    args = [q, kc, vc]
    if has_lat:
        in_specs += [spec(kl, kl.shape[-2], False), spec(vl, vl.shape[-2], False)]
        args += [kl, vl]
    lambda_init = 0.0
    if diff:
        lam_vecs, sub_g, lambda_init = diff_args
        in_specs += [_const_spec(lam_vecs.shape), _const_spec(sub_g.shape)]
        args += [lam_vecs, sub_g]
    return pl.pallas_call(
        functools.partial(_attn_kernel, diff=diff, has_lat=has_lat, tk=tk, lambda_init=lambda_init),
        grid=(b, lq // tq),
        in_specs=in_specs,
        out_specs=pl.BlockSpec((1, tq, GROUP_W), lambda i, j: (i, j, 0)),
        out_shape=jax.ShapeDtypeStruct((b, lq, GROUP_W), BF16),
        scratch_shapes=[pltpu.VMEM((N_H * (2 if diff else 1), tq, V_SLOT), F32)],
        compiler_params=_params("parallel", "parallel"),
        name=name,
    )(*args)


def _gla_chunk(q, k, v, a, st_ref, rev, consts):
    tri_f, tri_b, rexp, bdmask, rowid = consts
    c = GLA_CHUNK
    b = _dot(tri_b if rev else tri_f, a, precision=HIGHEST)
    qk_tiles = c // 8

    pieces, spans = [], []
    for j in range(c):
        t = j // 8
        t0, t1 = (0, t + 1) if rev else (t, qk_tiles)
        r0, r1 = 8 * t0, 8 * t1
        rid = rowid[r0:r1]
        keep = (rid <= j) if rev else (rid >= j)
        arg = jnp.where(keep, b[r0:r1] - b[j:j + 1], NEG_BIG)
        pieces.append(jnp.exp(arg) * (q[r0:r1] * k[j:j + 1]))
        spans.append((t0, t1))
    e_all = jnp.concatenate(pieces, axis=0).astype(BF16)
    sx = _dot(e_all, rexp)
    acc = [jnp.zeros((8, GROUP_W), F32) for _ in range(qk_tiles)]
    off = 0
    for j in range(c):
        t0, t1 = spans[j]
        vj = v[j:j + 1]
        for t in range(t0, t1):
            acc[t] = acc[t] + sx[off:off + 8] * vj
            off += 8
    o = jnp.concatenate(acc, axis=0)

    st = st_ref[...]
    o = o + _dot_nt((q * jnp.exp(b)).astype(BF16), st.astype(BF16))
    b_end = b[0:1] if rev else b[c - 1:c]
    ks = (k * jnp.exp(b_end - b)).astype(BF16)
    upd = _dot_tn(v.astype(BF16), ks)
    st_ref[...] = st * jnp.exp(b_end) + jnp.where(bdmask, upd, 0.0)
    return o


def _gla_kernel(qc_ref, kc_ref, vc_ref, afc_ref, abc_ref, ql_ref, kl_ref, vl_ref, afl_ref, abl_ref,
                oc_ref, ol_ref, sf_ref, sb_ref):
    c = GLA_CHUNK
    ri, ci = _iota((c, c), 0), _iota((c, c), 1)
    tri_f = jnp.where(ri >= ci, 1.0, 0.0).astype(F32)
    tri_b = jnp.where(ri <= ci, 1.0, 0.0).astype(F32)
    rexp = jnp.where((_iota((128, GROUP_W), 0) // GLA_DK) == (_iota((128, GROUP_W), 1) // HEAD_DIM),
                     1.0, 0.0).astype(BF16)
    bdmask = (_iota((GROUP_W, 128), 0) // HEAD_DIM) == (_iota((GROUP_W, 128), 1) // GLA_DK)
    rowid = _iota((c, 1), 0)
    consts = (tri_f, tri_b, rexp, bdmask, rowid)

    sf_ref[...] = jnp.zeros_like(sf_ref)
    sb_ref[...] = jnp.zeros_like(sb_ref)
    oc_ref[...] = jnp.zeros_like(oc_ref)
    ol_ref[...] = jnp.zeros_like(ol_ref)

    def sweep(q_ref, k_ref, v_ref, af_ref, ab_ref, o_ref):
        n = q_ref.shape[1] // c

        def body(i, carry):
            for rev, idx, a_ref, st_ref in ((False, i, af_ref, sf_ref), (True, n - 1 - i, ab_ref, sb_ref)):
                rows = pl.ds(pl.multiple_of(idx * c, c), c)
                o = _gla_chunk(q_ref[0, rows, :], k_ref[0, rows, :], v_ref[0, rows, :], a_ref[0, rows, :],
                               st_ref, rev, consts)
                o_ref[0, rows, :] += o
            return carry

        lax.fori_loop(0, n, body, 0)

    sweep(qc_ref, kc_ref, vc_ref, afc_ref, abc_ref, oc_ref)
    sweep(ql_ref, kl_ref, vl_ref, afl_ref, abl_ref, ol_ref)


def _seq_spec(arr):
    return pl.BlockSpec((1,) + arr.shape[1:], lambda i: (i, 0, 0))


def _gla_call(pc, plat):
    b = pc["gq"].shape[0]
    args = [pc["gq"], pc["gk"], pc["gv"], pc["gaf"], pc["gab"],
            plat["gq"], plat["gk"], plat["gv"], plat["gaf"], plat["gab"]]
    lc, ll = pc["gq"].shape[1], plat["gq"].shape[1]
    return pl.pallas_call(
        _gla_kernel,
        grid=(b,),
        in_specs=[_seq_spec(a) for a in args],
        out_specs=[pl.BlockSpec((1, lc, GROUP_W), lambda i: (i, 0, 0)),
                   pl.BlockSpec((1, ll, GROUP_W), lambda i: (i, 0, 0))],
        out_shape=[jax.ShapeDtypeStruct((b, lc, GROUP_W), F32), jax.ShapeDtypeStruct((b, ll, GROUP_W), F32)],
        scratch_shapes=[pltpu.VMEM((GROUP_W, 128), F32), pltpu.VMEM((GROUP_W, 128), F32)],
        compiler_params=_params("parallel"),
        name="gla_scan",
    )(*args)


def _ret_kernel(lg_ref, qc_ref, kc_ref, vc_ref, ql_ref, kl_ref, vl_ref, oc_ref, ol_ref,
                sf_ref, sb_ref, dm_ref):
    c = RET_CHUNK
    lgam = _log_sigmoid(lg_ref[...])
    ri, ci = _iota((c, c), 0), _iota((c, c), 1)
    pos = _iota((c, 128), 0).astype(F32)
    dq, dk, dc = [], [], []
    for d in range(2):
        lg = lgam[d:d + 1]
        dist = (ri - ci) if d == 0 else (ci - ri)
        for h in range(N_H):
            lg_h = lg[:, h * RET_DK:h * RET_DK + 1]
            dm_ref[d, h] = jnp.exp(jnp.where(dist >= 0, dist.astype(F32) * lg_h, NEG_BIG))
        dq.append(jnp.exp(((pos + 1.0) if d == 0 else (c - pos)) * lg))
        dk.append(jnp.exp(((c - 1.0 - pos) if d == 0 else pos) * lg))
        dc.append(jnp.exp(c * lg))
    bdmask = (_iota((GROUP_W, 128), 0) // HEAD_DIM) == (_iota((GROUP_W, 128), 1) // RET_DK)
    lane_q = _iota((1, 128), 1)
    lane_o = _iota((1, GROUP_W), 1)

    sf_ref[...] = jnp.zeros_like(sf_ref)
    sb_ref[...] = jnp.zeros_like(sb_ref)
    oc_ref[...] = jnp.zeros_like(oc_ref)
    ol_ref[...] = jnp.zeros_like(ol_ref)

    def chunk(q, k, v, st_ref, d):
        st = st_ref[...]
        k16, v16 = k.astype(BF16), v.astype(BF16)
        o = _dot_nt((q * dq[d]).astype(BF16), st.astype(BF16))
        for h in range(N_H):
            qh = jnp.where((lane_q // RET_DK) == h, q, 0.0).astype(BF16)
            s = _dot_nt(qh, k16) * dm_ref[d, h]
            o = o + jnp.where((lane_o // HEAD_DIM) == h, _dot(s.astype(BF16), v16), 0.0)
        upd = _dot_tn(v16, (k * dk[d]).astype(BF16))
        st_ref[...] = st * dc[d] + jnp.where(bdmask, upd, 0.0)
        return o

    def sweep(q_ref, k_ref, v_ref, o_ref):
        n = q_ref.shape[1] // c

        def body(i, carry):
            for d, idx, st_ref in ((0, i, sf_ref), (1, n - 1 - i, sb_ref)):
                rows = pl.ds(pl.multiple_of(idx * c, c), c)
                o_ref[0, rows, :] += chunk(q_ref[0, rows, :], k_ref[0, rows, :], v_ref[0, rows, :], st_ref, d)
            return carry

        lax.fori_loop(0, n, body, 0)

    sweep(qc_ref, kc_ref, vc_ref, oc_ref)
    sweep(ql_ref, kl_ref, vl_ref, ol_ref)


def _ret_call(lg, pc, plat):
    b = pc["rq"].shape[0]
    args = [pc["rq"], pc["rk"], pc["rv"], plat["rq"], plat["rk"], plat["rv"]]
    lc, ll = pc["rq"].shape[1], plat["rq"].shape[1]
    return pl.pallas_call(
        _ret_kernel,
        grid=(b,),
        in_specs=[_const_spec(lg.shape)] + [_seq_spec(a) for a in args],
        out_specs=[pl.BlockSpec((1, lc, GROUP_W), lambda i: (i, 0, 0)),
                   pl.BlockSpec((1, ll, GROUP_W), lambda i: (i, 0, 0))],
        out_shape=[jax.ShapeDtypeStruct((b, lc, GROUP_W), F32), jax.ShapeDtypeStruct((b, ll, GROUP_W), F32)],
        scratch_shapes=[pltpu.VMEM((GROUP_W, 128), F32), pltpu.VMEM((GROUP_W, 128), F32),
                        pltpu.VMEM((2, N_H, RET_CHUNK, RET_CHUNK), F32)],
        compiler_params=_params("parallel"),
        name="ret_scan",
    )(lg, *args)


def _mix_out_kernel(x_ref, mod_ref, ng_ref, do_ref, mo_ref, go_ref, gr_ref, ro_ref, rg_ref,
                    gg_ref, rgn_ref, wo_ref, x1_ref, h2_ref):
    gmat = _group_mean_matrix(GROUP_W, HEAD_DIM)
    gl = _group_norm(go_ref[0], gg_ref[...], gmat) * _silu(gr_ref[0])
    rt = _group_norm(ro_ref[0], rgn_ref[...], gmat) * _silu(rg_ref[0])
    ml = (_dot(do_ref[0], wo_ref[0:256]) + _dot(mo_ref[0], wo_ref[256:512])
          + _dot(gl.astype(BF16), wo_ref[512:768]) + _dot(rt.astype(BF16), wo_ref[768:1024]))
    x1 = x_ref[0] + mod_ref[0, 2:3] * _rms(ml, ng_ref[1:2])
    x1_ref[0] = x1
    h2_ref[0] = (_rms(x1, ng_ref[2:3]) * (1.0 + mod_ref[0, 4:5]) + mod_ref[0, 3:4]).astype(BF16)


def _mod_spec(mod, d):
    if mod.shape[0] > 1:
        return pl.BlockSpec((1, 6, d), lambda i, j: (i, 0, 0))
    return pl.BlockSpec((1, 6, d), lambda i, j: (0, 0, 0))


def _mix_out_call(x, mod, ng, d_o, m_o, g_o, g_r, r_o, r_g, lw, tm):
    b, n, d = x.shape

    def tok(w):
        return pl.BlockSpec((1, tm, w), lambda i, j: (i, j, 0))

    return pl.pallas_call(
        _mix_out_kernel,
        grid=(b, n // tm),
        in_specs=[tok(d), _mod_spec(mod, d), _const_spec(ng.shape)] + [tok(GROUP_W)] * 6
                 + [_const_spec(lw["gla_g"].shape), _const_spec(lw["ret_g"].shape), _const_spec(lw["w_out"].shape)],
        out_specs=[tok(d), tok(d)],
        out_shape=[jax.ShapeDtypeStruct((b, n, d), F32), jax.ShapeDtypeStruct((b, n, d), BF16)],
        compiler_params=_params("parallel", "parallel"),
        name="mix_out",
    )(x, mod, ng, d_o, m_o, g_o, g_r, r_o, r_g, lw["gla_g"], lw["ret_g"], lw["w_out"])


def _ffn_kernel(h_ref, x1_ref, mod_ref, ng_ref, wg_ref, wu_ref, wo_ref, x2_ref):
    h = h_ref[0]
    a = (_silu(_dot(h, wg_ref[...])) * _dot(h, wu_ref[...])).astype(BF16)
    y = _dot(a, wo_ref[...])
    x2_ref[0] = x1_ref[0] + mod_ref[0, 5:6] * _rms(y, ng_ref[3:4])


def _ffn_call(h2, x1, mod, ng, lw, tm):
    b, n, d = x1.shape

    def tok(w):
        return pl.BlockSpec((1, tm, w), lambda i, j: (i, j, 0))

    return pl.pallas_call(
        _ffn_kernel,
        grid=(b, n // tm),
        in_specs=[tok(d), tok(d), _mod_spec(mod, d), _const_spec(ng.shape),
                  _const_spec(lw["ffn_g"].shape), _const_spec(lw["ffn_u"].shape), _const_spec(lw["ffn_o"].shape)],
        out_specs=tok(d),
        out_shape=jax.ShapeDtypeStruct((b, n, d), F32),
        compiler_params=_params("parallel", "parallel"),
        name="ffn",
    )(h2, x1, mod, ng, lw["ffn_g"], lw["ffn_u"], lw["ffn_o"])


def _pack_layer(l, w_in, diff_norm, mla_q_norm, mla_kv_norm, mla_w_uq, mla_w_ukv, gla_w_gate, gla_b_gate,
                gla_norm, ret_decay, ret_norm, w_out, ffn_w_in, ffn_w_out):
    d = w_in.shape[1]
    w = w_in[l]
    sizes = (256, 256, 256, 256, 128, 32, 128, 128, 256, 256, 16, 16, 128, 128, 256, 256)
    offs = np.concatenate([[0], np.cumsum(sizes)])
    seg = [w[:, offs[i]:offs[i + 1]] for i in range(len(sizes))]
    z = lambda n: jnp.zeros((d, n), w.dtype)
    misc = jnp.concatenate([seg[10], seg[11], z(32), seg[5], z(32)], axis=1)
    wm = jnp.concatenate(seg[0:5] + [misc] + seg[6:10] + seg[12:16], axis=1).astype(BF16)

    uq = mla_w_uq[l].reshape(-1, N_H, MLA_NOPE + MLA_ROPE)
    uq = jnp.pad(uq, ((0, 0), (0, 0), (0, MLA_SLOT - MLA_NOPE - MLA_ROPE))).reshape(-1, N_H * MLA_SLOT)
    ukv = mla_w_ukv[l].reshape(-1, N_H, MLA_NOPE + HEAD_DIM)
    uk = jnp.pad(ukv[:, :, :MLA_NOPE], ((0, 0), (0, 0), (0, MLA_SLOT - MLA_NOPE))).reshape(-1, N_H * MLA_SLOT)
    uv = ukv[:, :, MLA_NOPE:].reshape(-1, GROUP_W)

    wgf = jnp.zeros((128, 128), F32).at[0:GLA_GATE_RANK].set(gla_w_gate[l, 0])
    wgb = jnp.zeros((128, 128), F32).at[GLA_GATE_RANK:2 * GLA_GATE_RANK].set(gla_w_gate[l, 1])
    hid = ffn_w_out.shape[1]
    return {
        "wm": wm, "qg": mla_q_norm[l][None], "kvg": mla_kv_norm[l][None],
        "wuq": uq.astype(BF16), "wuk": uk.astype(BF16), "wuv": uv.astype(BF16),
        "wgf": wgf, "wgb": wgb, "bg": gla_b_gate[l],
        "diff_g": jnp.tile(diff_norm[l], N_H)[None], "gla_g": jnp.tile(gla_norm[l], N_H)[None],
        "ret_g": jnp.tile(ret_norm[l], N_H)[None],
        "ret_lg": jnp.repeat(ret_decay[l], RET_DK, axis=1),
        "w_out": w_out[l].astype(BF16),
        "ffn_g": ffn_w_in[l][:, :hid].astype(BF16), "ffn_u": ffn_w_in[l][:, hid:].astype(BF16),
        "ffn_o": ffn_w_out[l].astype(BF16),
    }


def _rope_tables(seq):
    rows = seq // GRID_W
    row = jnp.repeat(jnp.arange(rows, dtype=F32), GRID_W)
    col = jnp.tile(jnp.arange(GRID_W, dtype=F32), rows)
    n_freq = ROT_DIM // 4
    freqs = ROPE_THETA ** (-jnp.arange(n_freq, dtype=F32) / n_freq)
    ang = jnp.concatenate([row[:, None] * freqs, col[:, None] * freqs], axis=-1)
    cos, sin = jnp.cos(ang), jnp.sin(ang)
    c32 = jnp.concatenate([cos, cos], axis=-1)
    s32 = jnp.concatenate([-sin, sin], axis=-1)
    one, zero = jnp.ones((seq, 1), F32), jnp.zeros((seq, 1), F32)
    slot_c = jnp.concatenate([jnp.tile(one, (1, MLA_NOPE)), c32, jnp.tile(one, (1, MLA_SLOT - MLA_NOPE - MLA_ROPE))], -1)
    slot_s = jnp.concatenate([jnp.tile(zero, (1, MLA_NOPE)), s32, jnp.tile(zero, (1, MLA_SLOT - MLA_NOPE - MLA_ROPE))], -1)
    return (jnp.tile(c32, (1, 256 // ROT_DIM)), jnp.tile(s32, (1, 256 // ROT_DIM)),
            jnp.tile(slot_c, (1, N_H)), jnp.tile(slot_s, (1, N_H)))


def _tile(n, want):
    t = min(n, want)
    while n % t:
        t //= 2
    return t


def kernel(x, c, ctx, c_ctx, ada_w, ada_b, norm_g, w_in, diff_lam, diff_norm, mla_q_norm, mla_kv_norm,
           mla_w_uq, mla_w_ukv, gla_w_gate, gla_b_gate, gla_norm, ret_decay, ret_norm, w_out, ffn_w_in, ffn_w_out):
    b, seq, d = x.shape
    lc = ctx.shape[1]
    depth = ada_w.shape[0]
    rope_tabs = _rope_tables(seq)

    pad = (-(b + 1)) % 8
    s_all = jnp.concatenate([c, c_ctx[None], jnp.zeros((pad, d), F32)], axis=0)
    mods = _ada_call(s_all, ada_w, ada_b)

    tm_l, tm_c = _tile(seq, 256), _tile(lc, 256)
    tq_l, tq_c = _tile(seq, 256), _tile(lc, 256)
    tk = _tile(seq, 512)

    xl, xc = x, ctx
    for l in range(depth):
        with_ctx = l < depth - 1
        lambda_init = 0.8 - 0.6 * math.exp(-0.3 * l)
        lw = _pack_layer(l, w_in, diff_norm, mla_q_norm, mla_kv_norm, mla_w_uq, mla_w_ukv, gla_w_gate,
                         gla_b_gate, gla_norm, ret_decay, ret_norm, w_out, ffn_w_in, ffn_w_out)
        mod_l = mods[l, :b].reshape(b, 6, d)
        mod_c = mods[l, b:b + 1].reshape(1, 6, d)
        ng = norm_g[l]

        pc = _proj_call(xc, mod_c, ng, lw, None, tm_c)
        pt = _proj_call(xl, mod_l, ng, lw, rope_tabs, tm_l)

        dargs = (diff_lam[l], lw["diff_g"], lambda_init)
        d_l = _attn_call(pt["dq"], pc["dk"], pc["dv"], pt["dk"], pt["dv"], dargs, tq_l, tk, "diff_attn")
        m_l = _attn_call(pt["mq"], pc["mk"], pc["mv"], pt["mk"], pt["mv"], None, tq_l, tk, "mla_attn")
        g_c, g_l = _gla_call(pc, pt)
        r_c, r_l = _ret_call(lw["ret_lg"], pc, pt)

        x1, h2 = _mix_out_call(xl, mod_l, ng, d_l, m_l, g_l, pt["gr"], r_l, pt["rg"], lw, tm_l)
        xl = _ffn_call(h2, x1, mod_l, ng, lw, tm_l)
        if with_ctx:
            d_c = _attn_call(pc["dq"], pc["dk"], pc["dv"], None, None, dargs, tq_c, tk, "diff_attn_ctx")
            m_c = _attn_call(pc["mq"], pc["mk"], pc["mv"], None, None, None, tq_c, tk, "mla_attn_ctx")
            x1c, h2c = _mix_out_call(xc, mod_c, ng, d_c, m_c, g_c, pc["gr"], r_c, pc["rg"], lw, tm_c)
            xc = _ffn_call(h2c, x1c, mod_c, ng, lw, tm_c)
    return xl
```

```python
import functools
import math

import jax
import jax.numpy as jnp
import numpy as np
from jax import lax
from jax.experimental import pallas as pl
from jax.experimental.pallas import tpu as pltpu

GRID_W = 64
N_H = 4
HEAD_DIM = 64
GROUP_W = N_H * HEAD_DIM
ROT_DIM = 32
ROPE_THETA = 10000.0
DIFF_QK = 32
MLA_NOPE = 64
MLA_ROPE = 32
MLA_SLOT = 128
V_SLOT = 128
V_ONES = HEAD_DIM
LOG2E = math.log2(math.e)
GLA_DK = 32
GLA_GATE_RANK = 16
GLA_GATE_NORM = 16.0
GLA_CHUNK = 64
GLA_SUB = 16
RET_DK = 32
RET_CHUNK = 256
RMS_EPS = 1e-6
NEG_BIG = -1e30

VMEM_LIMIT_BYTES = 48 * 1024 * 1024
F32 = jnp.float32
BF16 = jnp.bfloat16
HIGHEST = lax.Precision.HIGHEST

C_DQ, C_DK, C_DV, C_CQ, C_CKV, C_MISC = 0, 256, 512, 768, 1024, 1152
C_GQ, C_GK, C_GV, C_GR, C_RQ, C_RK, C_RV, C_RG, C_END = 1280, 1408, 1536, 1792, 2048, 2176, 2304, 2560, 2816
MISC_KR = 64


def _dot(a, b, precision=None):
    return jnp.dot(a, b, preferred_element_type=F32, precision=precision)


def _dot_nt(a, b):
    return lax.dot_general(a, b, (((1,), (1,)), ((), ())), preferred_element_type=F32)


def _dot_tn(a, b):
    return lax.dot_general(a, b, (((0,), (0,)), ((), ())), preferred_element_type=F32)


def _rms(x, g):
    ms = jnp.mean(x * x, axis=-1, keepdims=True)
    return x * lax.rsqrt(ms + RMS_EPS) * g


def _silu(x):
    return x * (1.0 / (1.0 + jnp.exp(-x)))


def _log_sigmoid(x):
    return jnp.minimum(x, 0.0) - jnp.log1p(jnp.exp(-jnp.abs(x)))


def _iota(shape, dim):
    return lax.broadcasted_iota(jnp.int32, shape, dim)


def _group_mean_matrix(width, group):
    same = (_iota((width, width), 0) // group) == (_iota((width, width), 1) // group)
    return jnp.where(same, 1.0 / group, 0.0).astype(BF16)


def _group_norm(o, g, gmat):
    ms = _dot((o * o).astype(BF16), gmat)
    return o * lax.rsqrt(ms + RMS_EPS) * g


def _rope(t, cs, sn):
    w = t.shape[-1]
    lo = (_iota((1, w), 1) % ROT_DIM) < (ROT_DIM // 2)
    swapped = jnp.where(lo, pltpu.roll(t, w - ROT_DIM // 2, 1), pltpu.roll(t, ROT_DIM // 2, 1))
    return t * cs + swapped * sn


def _const_spec(shape):
    nd = len(shape)
    return pl.BlockSpec(shape, lambda *_: (0,) * nd)


def _params(*sem):
    return pltpu.CompilerParams(dimension_semantics=sem, vmem_limit_bytes=VMEM_LIMIT_BYTES)


def _ada_kernel(s_ref, w_ref, b_ref, o_ref):
    s = _silu(s_ref[...])
    o_ref[0] = _dot(s, w_ref[0], precision=HIGHEST) + b_ref[0]


def _ada_call(s_all, ada_w, ada_b):
    depth, d, n = ada_w.shape
    rows = s_all.shape[0]
    tn = 1024
    return pl.pallas_call(
        _ada_kernel,
        grid=(depth, n // tn),
        in_specs=[pl.BlockSpec((rows, d), lambda l, j: (0, 0)),
                  pl.BlockSpec((1, d, tn), lambda l, j: (l, 0, j)),
                  pl.BlockSpec((1, 1, tn), lambda l, j: (l, 0, j))],
        out_specs=pl.BlockSpec((1, rows, tn), lambda l, j: (l, 0, j)),
        out_shape=jax.ShapeDtypeStruct((depth, rows, n), F32),
        compiler_params=_params("parallel", "parallel"),
        name="ada_mod",
    )(s_all, ada_w, ada_b.reshape(depth, 1, n))


def _proj_kernel(*refs, rotate):
    (x_ref, mod_ref, ng_ref, wm_ref, qg_ref, kvg_ref, wuq_ref, wuk_ref, wuv_ref,
     wgf_ref, wgb_ref, bg_ref) = refs[:12]
    refs = refs[12:]
    if rotate:
        cs256_ref, sn256_ref, cs512_ref, sn512_ref = refs[:4]
        refs = refs[4:]
    (dq_ref, dk_ref, dv_ref, mq_ref, mk_ref, mv_ref, gq_ref, gk_ref, gv_ref, gr_ref,
     gaf_ref, gab_ref, rq_ref, rk_ref, rv_ref, rg_ref) = refs

    x = x_ref[0]
    h = (_rms(x, ng_ref[0:1]) * (1.0 + mod_ref[0, 1:2]) + mod_ref[0, 0:1]).astype(BF16)

    def proj(lo, hi):
        return _dot(h, wm_ref[:, lo:hi])

    def rope256(t):
        return _rope(t, cs256_ref[...], sn256_ref[...]) if rotate else t

    def rope128(t):
        return _rope(t, cs256_ref[:, :128], sn256_ref[:, :128]) if rotate else t

    def rope512(t):
        return _rope(t, cs512_ref[...], sn512_ref[...]) if rotate else t

    lane = _iota((1, 128), 1)
    ones_lane = jnp.where(lane == V_ONES, 1.0, 0.0)

    def store_values(v_ref, v):
        rolled = pltpu.roll(v, GROUP_W - HEAD_DIM, 1)
        for hd in range(N_H):
            src = v if hd % 2 == 0 else rolled
            blk = src[:, (hd // 2) * 128:(hd // 2 + 1) * 128]
            v_ref[0, hd] = jnp.where(lane < HEAD_DIM, blk, ones_lane).astype(BF16)

    dq_ref[0] = (rope256(proj(C_DQ, C_DK)) * (DIFF_QK ** -0.5 * LOG2E)).astype(BF16)
    dk_ref[0] = rope256(proj(C_DK, C_DV)).astype(BF16)
    store_values(dv_ref, proj(C_DV, C_CQ))

    qn = _rms(proj(C_CQ, C_CKV), qg_ref[...]).astype(BF16)
    q512 = (rope512(_dot(qn, wuq_ref[...])) * ((MLA_NOPE + MLA_ROPE) ** -0.5 * LOG2E)).astype(BF16)
    kvn = _rms(proj(C_CKV, C_MISC), kvg_ref[...]).astype(BF16)
    misc = proj(C_MISC, C_GQ)
    kr = jnp.where((lane >= MISC_KR) & (lane < MISC_KR + MLA_ROPE), misc, 0.0)
    k512 = rope512(_dot(kvn, wuk_ref[...]) + jnp.concatenate([kr] * N_H, axis=1)).astype(BF16)
    for hd in range(N_H):
        mq_ref[0, hd] = q512[:, hd * MLA_SLOT:(hd + 1) * MLA_SLOT]
        mk_ref[0, hd] = k512[:, hd * MLA_SLOT:(hd + 1) * MLA_SLOT]
    store_values(mv_ref, _dot(kvn, wuv_ref[...]))

    gq_ref[0] = proj(C_GQ, C_GK) * GLA_DK ** -0.5
    gk_ref[0] = proj(C_GK, C_GV)
    gv_ref[0] = proj(C_GV, C_GR)
    gr_ref[0] = proj(C_GR, C_RQ)
    gaf_ref[0] = _log_sigmoid(_dot(misc, wgf_ref[...], precision=HIGHEST) + bg_ref[0:1]) / GLA_GATE_NORM
    gab_ref[0] = _log_sigmoid(_dot(misc, wgb_ref[...], precision=HIGHEST) + bg_ref[1:2]) / GLA_GATE_NORM

    rq_ref[0] = rope128(proj(C_RQ, C_RK))
    rk_ref[0] = rope128(proj(C_RK, C_RV) * RET_DK ** -0.5)
    rv_ref[0] = proj(C_RV, C_RG)
    rg_ref[0] = proj(C_RG, C_END)


_HEAD_MAJOR = ("mq", "mk", "dv", "mv")
_PROJ_OUT = (("dq", 256, BF16), ("dk", 256, BF16), ("dv", V_SLOT, BF16),
             ("mq", MLA_SLOT, BF16), ("mk", MLA_SLOT, BF16), ("mv", V_SLOT, BF16),
             ("gq", 128, F32), ("gk", 128, F32), ("gv", 256, F32), ("gr", 256, F32),
             ("gaf", 128, F32), ("gab", 128, F32),
             ("rq", 128, F32), ("rk", 128, F32), ("rv", 256, F32), ("rg", 256, F32))


def _proj_call(x, mod, ng, lw, rope_tabs, tm):
    b, n, d = x.shape
    rotate = rope_tabs is not None
    per_batch_mod = mod.shape[0] > 1
    weights = (lw["wm"], lw["qg"], lw["kvg"], lw["wuq"], lw["wuk"], lw["wuv"], lw["wgf"], lw["wgb"], lw["bg"])
    in_specs = [pl.BlockSpec((1, tm, d), lambda i, j: (i, j, 0)),
                pl.BlockSpec((1, 6, d), (lambda i, j: (i, 0, 0)) if per_batch_mod else (lambda i, j: (0, 0, 0))),
                _const_spec(ng.shape)]
    in_specs += [_const_spec(w.shape) for w in weights]
    args = [x, mod, ng, *weights]
    if rotate:
        in_specs += [pl.BlockSpec((tm, t.shape[1]), lambda i, j: (j, 0)) for t in rope_tabs]
        args += list(rope_tabs)
    out_specs, out_shape = [], []
    for name, w, dt in _PROJ_OUT:
        if name in _HEAD_MAJOR:
            out_specs.append(pl.BlockSpec((1, N_H, tm, w), lambda i, j: (i, 0, j, 0)))
            out_shape.append(jax.ShapeDtypeStruct((b, N_H, n, w), dt))
        else:
            out_specs.append(pl.BlockSpec((1, tm, w), lambda i, j: (i, j, 0)))
            out_shape.append(jax.ShapeDtypeStruct((b, n, w), dt))
    outs = pl.pallas_call(
        functools.partial(_proj_kernel, rotate=rotate),
        grid=(b, n // tm),
        in_specs=in_specs,
        out_specs=out_specs,
        out_shape=out_shape,
        compiler_params=_params("parallel", "parallel"),
        name="in_proj_rot" if rotate else "in_proj",
    )(*args)
    return {name: o for (name, _, _), o in zip(_PROJ_OUT, outs)}


def _attn_kernel(*refs, diff, has_lat, tk, lambda_init):
    q_ref, kc_ref, vc_ref = refs[:3]
    refs = refs[3:]
    if has_lat:
        kl_ref, vl_ref = refs[:2]
        refs = refs[2:]
    if diff:
        lam_ref, g_ref = refs[:2]
        refs = refs[2:]
    if diff:
        o_ref, acc_ref, qst_ref = refs
    else:
        o_ref, acc_ref = refs
    tq = o_ref.shape[1]
    n_maps = 2 if diff else 1

    if diff:
        q = q_ref[0]
        lane_q = _iota((1, q.shape[-1]), 1)
        for i in range(2 * N_H):
            qst_ref[i * tq:(i + 1) * tq, :] = jnp.where((lane_q // DIFF_QK) == i, q, jnp.zeros_like(q))

    def scores(k_ref, rows):
        if diff:
            return [(_dot_nt(qst_ref[...], k_ref[0, rows, :]), 0)]
        return [(_dot_nt(q_ref[0, h], k_ref[0, h, rows, :]), h * tq) for h in range(N_H)]

    def step(k_ref, v_ref, rows, ms):
        new = []
        for g, (s, r0) in enumerate(scores(k_ref, rows)):
            mx = jnp.max(s, axis=-1, keepdims=True)
            if ms is not None:
                mx = jnp.maximum(ms[g], mx)
                alpha = jnp.exp2(ms[g] - mx)
            p = jnp.exp2(s - mx).astype(BF16)
            hrows = n_maps * tq
            for j in range(s.shape[0] // hrows):
                h = (r0 + j * hrows) // hrows
                pv = _dot(p[j * hrows:(j + 1) * hrows], v_ref[0, h, rows, :])
                arows = slice(r0 + j * hrows, r0 + (j + 1) * hrows)
                if ms is None:
                    acc_ref[arows, :] = pv
                else:
                    acc_ref[arows, :] = alpha[j * hrows:(j + 1) * hrows] * acc_ref[arows, :] + pv
            new.append(mx)
        return tuple(new)

    ms = step(kc_ref, vc_ref, slice(None), None)
    if has_lat:
        def body(c, carry):
            return step(kl_ref, vl_ref, pl.ds(pl.multiple_of(c * tk, tk), tk), carry)

        lax.fori_loop(0, kl_ref.shape[-2] // tk, body, ms)

    def head_out(i):
        a = acc_ref[i * tq:(i + 1) * tq, :]
        return a[:, :HEAD_DIM] * (1.0 / a[:, V_ONES:V_ONES + 1])

    if diff:
        lv = lam_ref[...]
        lam = (jnp.exp(jnp.sum(lv[0:1] * lv[1:2], axis=-1, keepdims=True))
               - jnp.exp(jnp.sum(lv[2:3] * lv[3:4], axis=-1, keepdims=True)) + lambda_init)
        out = jnp.concatenate([head_out(2 * h) - lam * head_out(2 * h + 1) for h in range(N_H)], axis=1)
        out = _group_norm(out, g_ref[...], _group_mean_matrix(GROUP_W, HEAD_DIM)) * (1.0 - lambda_init)
    else:
        out = jnp.concatenate([head_out(h) for h in range(N_H)], axis=1)
    o_ref[0] = out.astype(o_ref.dtype)


def _attn_call(q, kc, vc, kl, vl, diff_args, tq, tk, name):
    has_lat = kl is not None
    diff = diff_args is not None
    b, lq = q.shape[0], q.shape[-2]

    def spec(arr, rows, tiled):
        if arr.ndim == 4:
            return pl.BlockSpec((1, N_H, rows, arr.shape[3]), (lambda i, j: (i, 0, j, 0)) if tiled else (lambda i, j: (i, 0, 0, 0)))
        return pl.BlockSpec((1, rows, arr.shape[2]), (lambda i, j: (i, j, 0)) if tiled else (lambda i, j: (i, 0, 0)))

    in_specs = [spec(q, tq, True), spec(kc, kc.shape[-2], False), spec(vc, vc.shape[-2], False)]
    args = [q, kc, vc]
    if has_lat:
        in_specs += [spec(kl, kl.shape[-2], False), spec(vl, vl.shape[-2], False)]
        args += [kl, vl]
    lambda_init = 0.0
    if diff:
        lam_vecs, sub_g, lambda_init = diff_args
        in_specs += [_const_spec(lam_vecs.shape), _const_spec(sub_g.shape)]
        args += [lam_vecs, sub_g]
    return pl.pallas_call(
        functools.partial(_attn_kernel, diff=diff, has_lat=has_lat, tk=tk, lambda_init=lambda_init),
        grid=(b, lq // tq),
        in_specs=in_specs,
        out_specs=pl.BlockSpec((1, tq, GROUP_W), lambda i, j: (i, j, 0)),
        out_shape=jax.ShapeDtypeStruct((b, lq, GROUP_W), BF16),
        scratch_shapes=[pltpu.VMEM((N_H * (2 if diff else 1) * tq, V_SLOT), F32)]
                       + ([pltpu.VMEM((2 * N_H * tq, q.shape[-1]), BF16)] if diff else []),
        compiler_params=_params("parallel", "parallel"),
        name=name,
    )(*args)


def _gla_chunks(work, consts):
    tri_f, tri_b, rexp, bdmask, rowid = consts
    c, sb = GLA_CHUNK, GLA_SUB
    qk_tiles, tiles_per_sb, n_sb = c // 8, sb // 8, c // sb
    lane_k = _iota((1, 128), 1)
    lane_o = _iota((1, GROUP_W), 1)

    bs = [_dot(tri_b if rev else tri_f, a, precision=HIGHEST) for (_, _, _, a, _, rev) in work]

    e_alls, spans_all, qms, kss, q_ins, k_ends, b_ends = [], [], [], [], [], [], []
    for (q, k, v, a, st, rev), b in zip(work, bs):
        pieces, spans = [], []
        for j in range(c):
            t, blk = j // 8, j // sb
            t0, t1 = (blk * tiles_per_sb, t + 1) if rev else (t, (blk + 1) * tiles_per_sb)
            r0, r1 = 8 * t0, 8 * t1
            rid = rowid[r0:r1]
            keep = (rid <= j) if rev else (rid >= j)
            arg = jnp.where(keep, b[r0:r1] - b[j:j + 1], NEG_BIG)
            pieces.append(jnp.exp(arg) * (q[r0:r1] * k[j:j + 1]))
            spans.append((t0, t1))
        e_alls.append(jnp.concatenate(pieces, axis=0).astype(BF16))
        spans_all.append(spans)
        qm_d, ks_d = [], []
        for blk in (range(n_sb - 1) if rev else range(1, n_sb)):
            r0 = blk * sb
            if rev:
                ref_row, seen = b[r0 + sb:r0 + sb + 1], rowid >= r0 + sb
            else:
                ref_row, seen = b[r0 - 1:r0], rowid < r0
            qs = q[r0:r0 + sb] * jnp.exp(b[r0:r0 + sb] - ref_row)
            ks_d.append((k * jnp.exp(jnp.where(seen, ref_row - b, NEG_BIG))).astype(BF16))
            qm_d.append(jnp.concatenate([jnp.where((lane_k // GLA_DK) == h, qs, 0.0) for h in range(N_H)],
                                        axis=0).astype(BF16))
        qms.append(qm_d)
        kss.append(ks_d)
        b_end = b[0:1] if rev else b[c - 1:c]
        b_ends.append(b_end)
        q_ins.append((q * jnp.exp(b)).astype(BF16))
        k_ends.append((k * jnp.exp(b_end - b)).astype(BF16))

    sxs = [_dot(e_all, rexp) for e_all in e_alls]
    s_ods = [jnp.concatenate([_dot_nt(qm, ks) for qm, ks in zip(qm_d, ks_d)], axis=0).astype(BF16)
             for qm_d, ks_d in zip(qms, kss)]
    o_ins = [_dot_nt(q_in, w[4].astype(BF16)) for q_in, w in zip(q_ins, work)]
    upds = [_dot_tn(w[2].astype(BF16), k_end) for k_end, w in zip(k_ends, work)]

    ress = [_dot(s_od, w[2].astype(BF16)) for s_od, w in zip(s_ods, work)]

    outs = []
    for d, (q, k, v, a, st, rev) in enumerate(work):
        acc = [jnp.zeros((8, GROUP_W), F32) for _ in range(qk_tiles)]
        off = 0
        for j in range(c):
            t0, t1 = spans_all[d][j]
            vj = v[j:j + 1]
            for t in range(t0, t1):
                acc[t] = acc[t] + sxs[d][off:off + 8] * vj
                off += 8
        for n, blk in enumerate(range(n_sb - 1) if rev else range(1, n_sb)):
            o_blk = jnp.zeros((sb, GROUP_W), F32)
            for h in range(N_H):
                r0 = (n * N_H + h) * sb
                o_blk = o_blk + jnp.where((lane_o // HEAD_DIM) == h, ress[d][r0:r0 + sb], 0.0)
            for tt in range(tiles_per_sb):
                t = blk * tiles_per_sb + tt
                acc[t] = acc[t] + o_blk[8 * tt:8 * tt + 8]
        o = jnp.concatenate(acc, axis=0) + o_ins[d]
        outs.append((o, st * jnp.exp(b_ends[d]) + jnp.where(bdmask, upds[d], 0.0)))
    return outs


def _gla_kernel(qc_ref, kc_ref, vc_ref, afc_ref, abc_ref, ql_ref, kl_ref, vl_ref, afl_ref, abl_ref,
                oc_ref, ol_ref, sf_ref, sb_ref):
    c = GLA_CHUNK
    ri, ci = _iota((c, c), 0), _iota((c, c), 1)
    tri_f = jnp.where(ri >= ci, 1.0, 0.0).astype(F32)
    tri_b = jnp.where(ri <= ci, 1.0, 0.0).astype(F32)
    rexp = jnp.where((_iota((128, GROUP_W), 0) // GLA_DK) == (_iota((128, GROUP_W), 1) // HEAD_DIM),
                     1.0, 0.0).astype(BF16)
    bdmask = (_iota((GROUP_W, 128), 0) // HEAD_DIM) == (_iota((GROUP_W, 128), 1) // GLA_DK)
    rowid = _iota((c, 1), 0)
    consts = (tri_f, tri_b, rexp, bdmask, rowid)

    sf_ref[...] = jnp.zeros_like(sf_ref)
    sb_ref[...] = jnp.zeros_like(sb_ref)
    oc_ref[...] = jnp.zeros_like(oc_ref)
    ol_ref[...] = jnp.zeros_like(ol_ref)

    def sweep(q_ref, k_ref, v_ref, af_ref, ab_ref, o_ref):
        n = q_ref.shape[1] // c

        def body(i, carry):
            rows_f = pl.ds(pl.multiple_of(i * c, c), c)
            rows_b = pl.ds(pl.multiple_of((n - 1 - i) * c, c), c)
            work = [(q_ref[0, rows, :], k_ref[0, rows, :], v_ref[0, rows, :], a_ref[0, rows, :], st_ref[...], rev)
                    for rows, a_ref, st_ref, rev in ((rows_f, af_ref, sf_ref, False), (rows_b, ab_ref, sb_ref, True))]
            (o_f, st_f), (o_b, st_b) = _gla_chunks(work, consts)
            sf_ref[...] = st_f
            sb_ref[...] = st_b
            o_ref[0, rows_f, :] += o_f
            o_ref[0, rows_b, :] += o_b
            return carry

        lax.fori_loop(0, n, body, 0)

    sweep(qc_ref, kc_ref, vc_ref, afc_ref, abc_ref, oc_ref)
    sweep(ql_ref, kl_ref, vl_ref, afl_ref, abl_ref, ol_ref)


def _seq_spec(arr):
    return pl.BlockSpec((1,) + arr.shape[1:], lambda i: (i, 0, 0))


def _gla_call(pc, plat):
    b = pc["gq"].shape[0]
    args = [pc["gq"], pc["gk"], pc["gv"], pc["gaf"], pc["gab"],
            plat["gq"], plat["gk"], plat["gv"], plat["gaf"], plat["gab"]]
    lc, ll = pc["gq"].shape[1], plat["gq"].shape[1]
    return pl.pallas_call(
        _gla_kernel,
        grid=(b,),
        in_specs=[_seq_spec(a) for a in args],
        out_specs=[pl.BlockSpec((1, lc, GROUP_W), lambda i: (i, 0, 0)),
                   pl.BlockSpec((1, ll, GROUP_W), lambda i: (i, 0, 0))],
        out_shape=[jax.ShapeDtypeStruct((b, lc, GROUP_W), F32), jax.ShapeDtypeStruct((b, ll, GROUP_W), F32)],
        scratch_shapes=[pltpu.VMEM((GROUP_W, 128), F32), pltpu.VMEM((GROUP_W, 128), F32)],
        compiler_params=_params("parallel"),
        name="gla_scan",
    )(*args)


def _ret_kernel(lg_ref, qc_ref, kc_ref, vc_ref, ql_ref, kl_ref, vl_ref, oc_ref, ol_ref,
                sf_ref, sb_ref, dm_ref):
    c = RET_CHUNK
    lgam = _log_sigmoid(lg_ref[...])
    ri, ci = _iota((c, c), 0), _iota((c, c), 1)
    pos = _iota((c, 128), 0).astype(F32)
    dq, dk, dc = [], [], []
    for d in range(2):
        lg = lgam[d:d + 1]
        dist = (ri - ci) if d == 0 else (ci - ri)
        for h in range(N_H):
            lg_h = lg[:, h * RET_DK:h * RET_DK + 1]
            dm_ref[d, h] = jnp.exp(jnp.where(dist >= 0, dist.astype(F32) * lg_h, NEG_BIG))
        dq.append(jnp.exp(((pos + 1.0) if d == 0 else (c - pos)) * lg))
        dk.append(jnp.exp(((c - 1.0 - pos) if d == 0 else pos) * lg))
        dc.append(jnp.exp(c * lg))
    bdmask = (_iota((GROUP_W, 128), 0) // HEAD_DIM) == (_iota((GROUP_W, 128), 1) // RET_DK)
    lane_q = _iota((1, 128), 1)
    lane_o = _iota((1, GROUP_W), 1)

    sf_ref[...] = jnp.zeros_like(sf_ref)
    sb_ref[...] = jnp.zeros_like(sb_ref)
    oc_ref[...] = jnp.zeros_like(oc_ref)
    ol_ref[...] = jnp.zeros_like(ol_ref)

    def chunk(q, k, v, st_ref, d):
        st = st_ref[...]
        k16, v16 = k.astype(BF16), v.astype(BF16)
        o = _dot_nt((q * dq[d]).astype(BF16), st.astype(BF16))
        for h in range(N_H):
            qh = jnp.where((lane_q // RET_DK) == h, q, 0.0).astype(BF16)
            s = _dot_nt(qh, k16) * dm_ref[d, h]
            o = o + jnp.where((lane_o // HEAD_DIM) == h, _dot(s.astype(BF16), v16), 0.0)
        upd = _dot_tn(v16, (k * dk[d]).astype(BF16))
        st_ref[...] = st * dc[d] + jnp.where(bdmask, upd, 0.0)
        return o

    def sweep(q_ref, k_ref, v_ref, o_ref):
        n = q_ref.shape[1] // c

        def body(i, carry):
            for d, idx, st_ref in ((0, i, sf_ref), (1, n - 1 - i, sb_ref)):
                rows = pl.ds(pl.multiple_of(idx * c, c), c)
                o_ref[0, rows, :] += chunk(q_ref[0, rows, :], k_ref[0, rows, :], v_ref[0, rows, :], st_ref, d)
            return carry

        lax.fori_loop(0, n, body, 0)

    sweep(qc_ref, kc_ref, vc_ref, oc_ref)
    sweep(ql_ref, kl_ref, vl_ref, ol_ref)


def _ret_call(lg, pc, plat):
    b = pc["rq"].shape[0]
    args = [pc["rq"], pc["rk"], pc["rv"], plat["rq"], plat["rk"], plat["rv"]]
    lc, ll = pc["rq"].shape[1], plat["rq"].shape[1]
    return pl.pallas_call(
        _ret_kernel,
        grid=(b,),
        in_specs=[_const_spec(lg.shape)] + [_seq_spec(a) for a in args],
        out_specs=[pl.BlockSpec((1, lc, GROUP_W), lambda i: (i, 0, 0)),
                   pl.BlockSpec((1, ll, GROUP_W), lambda i: (i, 0, 0))],
        out_shape=[jax.ShapeDtypeStruct((b, lc, GROUP_W), F32), jax.ShapeDtypeStruct((b, ll, GROUP_W), F32)],
        scratch_shapes=[pltpu.VMEM((GROUP_W, 128), F32), pltpu.VMEM((GROUP_W, 128), F32),
                        pltpu.VMEM((2, N_H, RET_CHUNK, RET_CHUNK), F32)],
        compiler_params=_params("parallel"),
        name="ret_scan",
    )(lg, *args)


def _mix_out_kernel(x_ref, mod_ref, ng_ref, do_ref, mo_ref, go_ref, gr_ref, ro_ref, rg_ref,
                    gg_ref, rgn_ref, wo_ref, x1_ref, h2_ref):
    gmat = _group_mean_matrix(GROUP_W, HEAD_DIM)
    gl = _group_norm(go_ref[0], gg_ref[...], gmat) * _silu(gr_ref[0])
    rt = _group_norm(ro_ref[0], rgn_ref[...], gmat) * _silu(rg_ref[0])
    ml = (_dot(do_ref[0], wo_ref[0:256]) + _dot(mo_ref[0], wo_ref[256:512])
          + _dot(gl.astype(BF16), wo_ref[512:768]) + _dot(rt.astype(BF16), wo_ref[768:1024]))
    x1 = x_ref[0] + mod_ref[0, 2:3] * _rms(ml, ng_ref[1:2])
    x1_ref[0] = x1
    h2_ref[0] = (_rms(x1, ng_ref[2:3]) * (1.0 + mod_ref[0, 4:5]) + mod_ref[0, 3:4]).astype(BF16)


def _mod_spec(mod, d):
    if mod.shape[0] > 1:
        return pl.BlockSpec((1, 6, d), lambda i, j: (i, 0, 0))
    return pl.BlockSpec((1, 6, d), lambda i, j: (0, 0, 0))


def _mix_out_call(x, mod, ng, d_o, m_o, g_o, g_r, r_o, r_g, lw, tm):
    b, n, d = x.shape

    def tok(w):
        return pl.BlockSpec((1, tm, w), lambda i, j: (i, j, 0))

    return pl.pallas_call(
        _mix_out_kernel,
        grid=(b, n // tm),
        in_specs=[tok(d), _mod_spec(mod, d), _const_spec(ng.shape)] + [tok(GROUP_W)] * 6
                 + [_const_spec(lw["gla_g"].shape), _const_spec(lw["ret_g"].shape), _const_spec(lw["w_out"].shape)],
        out_specs=[tok(d), tok(d)],
        out_shape=[jax.ShapeDtypeStruct((b, n, d), F32), jax.ShapeDtypeStruct((b, n, d), BF16)],
        compiler_params=_params("parallel", "parallel"),
        name="mix_out",
    )(x, mod, ng, d_o, m_o, g_o, g_r, r_o, r_g, lw["gla_g"], lw["ret_g"], lw["w_out"])


def _ffn_kernel(h_ref, x1_ref, mod_ref, ng_ref, wg_ref, wu_ref, wo_ref, x2_ref):
    h = h_ref[0]
    a = (_silu(_dot(h, wg_ref[...])) * _dot(h, wu_ref[...])).astype(BF16)
    y = _dot(a, wo_ref[...])
    x2_ref[0] = x1_ref[0] + mod_ref[0, 5:6] * _rms(y, ng_ref[3:4])


def _ffn_call(h2, x1, mod, ng, lw, tm):
    b, n, d = x1.shape

    def tok(w):
        return pl.BlockSpec((1, tm, w), lambda i, j: (i, j, 0))

    return pl.pallas_call(
        _ffn_kernel,
        grid=(b, n // tm),
        in_specs=[tok(d), tok(d), _mod_spec(mod, d), _const_spec(ng.shape),
                  _const_spec(lw["ffn_g"].shape), _const_spec(lw["ffn_u"].shape), _const_spec(lw["ffn_o"].shape)],
        out_specs=tok(d),
        out_shape=jax.ShapeDtypeStruct((b, n, d), F32),
        compiler_params=_params("parallel", "parallel"),
        name="ffn",
    )(h2, x1, mod, ng, lw["ffn_g"], lw["ffn_u"], lw["ffn_o"])


def _pack_layer(l, w_in, diff_norm, mla_q_norm, mla_kv_norm, mla_w_uq, mla_w_ukv, gla_w_gate, gla_b_gate,
                gla_norm, ret_decay, ret_norm, w_out, ffn_w_in, ffn_w_out):
    d = w_in.shape[1]
    w = w_in[l]
    sizes = (256, 256, 256, 256, 128, 32, 128, 128, 256, 256, 16, 16, 128, 128, 256, 256)
    offs = np.concatenate([[0], np.cumsum(sizes)])
    seg = [w[:, offs[i]:offs[i + 1]] for i in range(len(sizes))]
    z = lambda n: jnp.zeros((d, n), w.dtype)
    misc = jnp.concatenate([seg[10], seg[11], z(32), seg[5], z(32)], axis=1)
    wm = jnp.concatenate(seg[0:5] + [misc] + seg[6:10] + seg[12:16], axis=1).astype(BF16)

    uq = mla_w_uq[l].reshape(-1, N_H, MLA_NOPE + MLA_ROPE)
    uq = jnp.pad(uq, ((0, 0), (0, 0), (0, MLA_SLOT - MLA_NOPE - MLA_ROPE))).reshape(-1, N_H * MLA_SLOT)
    ukv = mla_w_ukv[l].reshape(-1, N_H, MLA_NOPE + HEAD_DIM)
    uk = jnp.pad(ukv[:, :, :MLA_NOPE], ((0, 0), (0, 0), (0, MLA_SLOT - MLA_NOPE))).reshape(-1, N_H * MLA_SLOT)
    uv = ukv[:, :, MLA_NOPE:].reshape(-1, GROUP_W)

    wgf = jnp.zeros((128, 128), F32).at[0:GLA_GATE_RANK].set(gla_w_gate[l, 0])
    wgb = jnp.zeros((128, 128), F32).at[GLA_GATE_RANK:2 * GLA_GATE_RANK].set(gla_w_gate[l, 1])
    hid = ffn_w_out.shape[1]
    return {
        "wm": wm, "qg": mla_q_norm[l][None], "kvg": mla_kv_norm[l][None],
        "wuq": uq.astype(BF16), "wuk": uk.astype(BF16), "wuv": uv.astype(BF16),
        "wgf": wgf, "wgb": wgb, "bg": gla_b_gate[l],
        "diff_g": jnp.tile(diff_norm[l], N_H)[None], "gla_g": jnp.tile(gla_norm[l], N_H)[None],
        "ret_g": jnp.tile(ret_norm[l], N_H)[None],
        "ret_lg": jnp.repeat(ret_decay[l], RET_DK, axis=1),
        "w_out": w_out[l].astype(BF16),
        "ffn_g": ffn_w_in[l][:, :hid].astype(BF16), "ffn_u": ffn_w_in[l][:, hid:].astype(BF16),
        "ffn_o": ffn_w_out[l].astype(BF16),
    }


def _rope_tables(seq):
    rows = seq // GRID_W
    row = jnp.repeat(jnp.arange(rows, dtype=F32), GRID_W)
    col = jnp.tile(jnp.arange(GRID_W, dtype=F32), rows)
    n_freq = ROT_DIM // 4
    freqs = ROPE_THETA ** (-jnp.arange(n_freq, dtype=F32) / n_freq)
    ang = jnp.concatenate([row[:, None] * freqs, col[:, None] * freqs], axis=-1)
    cos, sin = jnp.cos(ang), jnp.sin(ang)
    c32 = jnp.concatenate([cos, cos], axis=-1)
    s32 = jnp.concatenate([-sin, sin], axis=-1)
    one, zero = jnp.ones((seq, 1), F32), jnp.zeros((seq, 1), F32)
    slot_c = jnp.concatenate([jnp.tile(one, (1, MLA_NOPE)), c32, jnp.tile(one, (1, MLA_SLOT - MLA_NOPE - MLA_ROPE))], -1)
    slot_s = jnp.concatenate([jnp.tile(zero, (1, MLA_NOPE)), s32, jnp.tile(zero, (1, MLA_SLOT - MLA_NOPE - MLA_ROPE))], -1)
    return (jnp.tile(c32, (1, 256 // ROT_DIM)), jnp.tile(s32, (1, 256 // ROT_DIM)),
            jnp.tile(slot_c, (1, N_H)), jnp.tile(slot_s, (1, N_H)))


def _tile(n, want):
    t = min(n, want)
    while n % t:
        t //= 2
    return t


def kernel(x, c, ctx, c_ctx, ada_w, ada_b, norm_g, w_in, diff_lam, diff_norm, mla_q_norm, mla_kv_norm,
           mla_w_uq, mla_w_ukv, gla_w_gate, gla_b_gate, gla_norm, ret_decay, ret_norm, w_out, ffn_w_in, ffn_w_out):
    b, seq, d = x.shape
    lc = ctx.shape[1]
    depth = ada_w.shape[0]
    rope_tabs = _rope_tables(seq)

    pad = (-(b + 1)) % 8
    s_all = jnp.concatenate([c, c_ctx[None], jnp.zeros((pad, d), F32)], axis=0)
    mods = _ada_call(s_all, ada_w, ada_b)

    tm_l, tm_c = _tile(seq, 256), _tile(lc, 256)
    tq_c = _tile(lc, 256)
    tq_d, tk_d = _tile(seq, 256), _tile(seq, 1024)
    tq_m, tk_m = _tile(seq, 512), _tile(seq, 1024)

    xl, xc = x, ctx
    for l in range(depth):
        with_ctx = l < depth - 1
        lambda_init = 0.8 - 0.6 * math.exp(-0.3 * l)
        lw = _pack_layer(l, w_in, diff_norm, mla_q_norm, mla_kv_norm, mla_w_uq, mla_w_ukv, gla_w_gate,
                         gla_b_gate, gla_norm, ret_decay, ret_norm, w_out, ffn_w_in, ffn_w_out)
        mod_l = mods[l, :b].reshape(b, 6, d)
        mod_c = mods[l, b:b + 1].reshape(1, 6, d)
        ng = norm_g[l]

        pc = _proj_call(xc, mod_c, ng, lw, None, tm_c)
        pt = _proj_call(xl, mod_l, ng, lw, rope_tabs, tm_l)

        dargs = (diff_lam[l], lw["diff_g"], lambda_init)
        d_l = _attn_call(pt["dq"], pc["dk"], pc["dv"], pt["dk"], pt["dv"], dargs, tq_d, tk_d, "diff_attn")
        m_l = _attn_call(pt["mq"], pc["mk"], pc["mv"], pt["mk"], pt["mv"], None, tq_m, tk_m, "mla_attn")
        g_c, g_l = _gla_call(pc, pt)
        r_c, r_l = _ret_call(lw["ret_lg"], pc, pt)

        x1, h2 = _mix_out_call(xl, mod_l, ng, d_l, m_l, g_l, pt["gr"], r_l, pt["rg"], lw, tm_l)
        xl = _ffn_call(h2, x1, mod_l, ng, lw, tm_l)
        if with_ctx:
            d_c = _attn_call(pc["dq"], pc["dk"], pc["dv"], None, None, dargs, tq_c, tk_d, "diff_attn_ctx")
            m_c = _attn_call(pc["mq"], pc["mk"], pc["mv"], None, None, None, tq_c, tk_m, "mla_attn_ctx")
            x1c, h2c = _mix_out_call(xc, mod_c, ng, d_c, m_c, g_c, pc["gr"], r_c, pc["rg"], lw, tm_c)
            xc = _ffn_call(h2c, x1c, mod_c, ng, lw, tm_c)
    return xl
```

```python
import functools
import math

import jax
import jax.numpy as jnp
import numpy as np
from jax import lax
from jax.experimental import pallas as pl
from jax.experimental.pallas import tpu as pltpu

GRID_W = 64
N_H = 4
HEAD_DIM = 64
GROUP_W = N_H * HEAD_DIM
ROT_DIM = 32
ROPE_THETA = 10000.0
DIFF_QK = 32
MLA_NOPE = 64
MLA_ROPE = 32
MLA_SLOT = 128
V_SLOT = 128
V_ONES = HEAD_DIM
LOG2E = math.log2(math.e)
GLA_DK = 32
GLA_GATE_RANK = 16
GLA_GATE_NORM = 16.0
GLA_CHUNK = 64
GLA_SUB = 16
RET_DK = 32
RET_CHUNK = 256
RMS_EPS = 1e-6
NEG_BIG = -1e30

VMEM_LIMIT_BYTES = 48 * 1024 * 1024
F32 = jnp.float32
BF16 = jnp.bfloat16
HIGHEST = lax.Precision.HIGHEST

C_DQ, C_DK, C_DV, C_CQ, C_CKV, C_MISC = 0, 256, 512, 768, 1024, 1152
C_GQ, C_GK, C_GV, C_GR, C_RQ, C_RK, C_RV, C_RG, C_END = 1280, 1408, 1536, 1792, 2048, 2176, 2304, 2560, 2816
MISC_KR = 64


def _dot(a, b, precision=None):
    return jnp.dot(a, b, preferred_element_type=F32, precision=precision)


def _dot_nt(a, b):
    return lax.dot_general(a, b, (((1,), (1,)), ((), ())), preferred_element_type=F32)


def _dot_tn(a, b):
    return lax.dot_general(a, b, (((0,), (0,)), ((), ())), preferred_element_type=F32)


def _rms(x, g):
    ms = jnp.mean(x * x, axis=-1, keepdims=True)
    return x * lax.rsqrt(ms + RMS_EPS) * g


def _silu(x):
    return x * (1.0 / (1.0 + jnp.exp(-x)))


def _log_sigmoid(x):
    return jnp.minimum(x, 0.0) - jnp.log1p(jnp.exp(-jnp.abs(x)))


def _iota(shape, dim):
    return lax.broadcasted_iota(jnp.int32, shape, dim)


def _group_mean_matrix(width, group):
    same = (_iota((width, width), 0) // group) == (_iota((width, width), 1) // group)
    return jnp.where(same, 1.0 / group, 0.0).astype(BF16)


def _group_norm(o, g, gmat):
    ms = _dot((o * o).astype(BF16), gmat)
    return o * lax.rsqrt(ms + RMS_EPS) * g


def _rope(t, cs, sn):
    w = t.shape[-1]
    lo = (_iota((1, w), 1) % ROT_DIM) < (ROT_DIM // 2)
    swapped = jnp.where(lo, pltpu.roll(t, w - ROT_DIM // 2, 1), pltpu.roll(t, ROT_DIM // 2, 1))
    return t * cs + swapped * sn


def _const_spec(shape):
    nd = len(shape)
    return pl.BlockSpec(shape, lambda *_: (0,) * nd)


def _params(*sem):
    return pltpu.CompilerParams(dimension_semantics=sem, vmem_limit_bytes=VMEM_LIMIT_BYTES)


def _ada_kernel(s_ref, w_ref, b_ref, o_ref):
    s = _silu(s_ref[...])
    o_ref[0] = _dot(s, w_ref[0], precision=HIGHEST) + b_ref[0]


def _ada_call(s_all, ada_w, ada_b):
    depth, d, n = ada_w.shape
    rows = s_all.shape[0]
    tn = 1024
    return pl.pallas_call(
        _ada_kernel,
        grid=(depth, n // tn),
        in_specs=[pl.BlockSpec((rows, d), lambda l, j: (0, 0)),
                  pl.BlockSpec((1, d, tn), lambda l, j: (l, 0, j)),
                  pl.BlockSpec((1, 1, tn), lambda l, j: (l, 0, j))],
        out_specs=pl.BlockSpec((1, rows, tn), lambda l, j: (l, 0, j)),
        out_shape=jax.ShapeDtypeStruct((depth, rows, n), F32),
        compiler_params=_params("parallel", "parallel"),
        name="ada_mod",
    )(s_all, ada_w, ada_b.reshape(depth, 1, n))


def _proj_kernel(*refs, rotate):
    (x_ref, mod_ref, ng_ref, wm_ref, qg_ref, kvg_ref, wuq_ref, wuk_ref, wuv_ref,
     wgf_ref, wgb_ref, bg_ref) = refs[:12]
    refs = refs[12:]
    if rotate:
        cs256_ref, sn256_ref, cs512_ref, sn512_ref = refs[:4]
        refs = refs[4:]
    (dq_ref, dk_ref, dv_ref, mq_ref, mk_ref, mv_ref, gq_ref, gk_ref, gv_ref, gr_ref,
     gaf_ref, gab_ref, rq_ref, rk_ref, rv_ref, rg_ref) = refs

    x = x_ref[0]
    h = (_rms(x, ng_ref[0:1]) * (1.0 + mod_ref[0, 1:2]) + mod_ref[0, 0:1]).astype(BF16)

    def proj(lo, hi):
        return _dot(h, wm_ref[:, lo:hi])

    def rope256(t):
        return _rope(t, cs256_ref[...], sn256_ref[...]) if rotate else t

    def rope128(t):
        return _rope(t, cs256_ref[:, :128], sn256_ref[:, :128]) if rotate else t

    def rope512(t):
        return _rope(t, cs512_ref[...], sn512_ref[...]) if rotate else t

    lane = _iota((1, 128), 1)
    ones_lane = jnp.where(lane == V_ONES, 1.0, 0.0)

    def store_values(v_ref, v):
        rolled = pltpu.roll(v, GROUP_W - HEAD_DIM, 1)
        for hd in range(N_H):
            src = v if hd % 2 == 0 else rolled
            blk = src[:, (hd // 2) * 128:(hd // 2 + 1) * 128]
            v_ref[0, hd] = jnp.where(lane < HEAD_DIM, blk, ones_lane).astype(BF16)

    dq_ref[0] = (rope256(proj(C_DQ, C_DK)) * (DIFF_QK ** -0.5 * LOG2E)).astype(BF16)
    dk_ref[0] = rope256(proj(C_DK, C_DV)).astype(BF16)
    store_values(dv_ref, proj(C_DV, C_CQ))

    qn = _rms(proj(C_CQ, C_CKV), qg_ref[...]).astype(BF16)
    q512 = (rope512(_dot(qn, wuq_ref[...])) * ((MLA_NOPE + MLA_ROPE) ** -0.5 * LOG2E)).astype(BF16)
    kvn = _rms(proj(C_CKV, C_MISC), kvg_ref[...]).astype(BF16)
    misc = proj(C_MISC, C_GQ)
    kr = jnp.where((lane >= MISC_KR) & (lane < MISC_KR + MLA_ROPE), misc, 0.0)
    k512 = rope512(_dot(kvn, wuk_ref[...]) + jnp.concatenate([kr] * N_H, axis=1)).astype(BF16)
    for hd in range(N_H):
        mq_ref[0, hd] = q512[:, hd * MLA_SLOT:(hd + 1) * MLA_SLOT]
        mk_ref[0, hd] = k512[:, hd * MLA_SLOT:(hd + 1) * MLA_SLOT]
    store_values(mv_ref, _dot(kvn, wuv_ref[...]))

    gq_ref[0] = proj(C_GQ, C_GK) * GLA_DK ** -0.5
    gk_ref[0] = proj(C_GK, C_GV)
    gv_ref[0] = proj(C_GV, C_GR).astype(BF16)
    gr_ref[0] = proj(C_GR, C_RQ).astype(BF16)
    misc16 = misc.astype(BF16)
    gaf_ref[0] = _log_sigmoid(_dot(misc16, wgf_ref[...]) + bg_ref[0:1]) / GLA_GATE_NORM
    gab_ref[0] = _log_sigmoid(_dot(misc16, wgb_ref[...]) + bg_ref[1:2]) / GLA_GATE_NORM

    rq_ref[0] = rope128(proj(C_RQ, C_RK))
    rk_ref[0] = rope128(proj(C_RK, C_RV) * RET_DK ** -0.5)
    rv_ref[0] = proj(C_RV, C_RG).astype(BF16)
    rg_ref[0] = proj(C_RG, C_END).astype(BF16)


_HEAD_MAJOR = ("mq", "mk", "dv", "mv")
_PROJ_OUT = (("dq", 256, BF16), ("dk", 256, BF16), ("dv", V_SLOT, BF16),
             ("mq", MLA_SLOT, BF16), ("mk", MLA_SLOT, BF16), ("mv", V_SLOT, BF16),
             ("gq", 128, F32), ("gk", 128, F32), ("gv", 256, BF16), ("gr", 256, BF16),
             ("gaf", 128, F32), ("gab", 128, F32),
             ("rq", 128, F32), ("rk", 128, F32), ("rv", 256, BF16), ("rg", 256, BF16))


def _proj_call(x, mod, ng, lw, rope_tabs, tm):
    b, n, d = x.shape
    rotate = rope_tabs is not None
    per_batch_mod = mod.shape[0] > 1
    weights = (lw["wm"], lw["qg"], lw["kvg"], lw["wuq"], lw["wuk"], lw["wuv"], lw["wgf"], lw["wgb"], lw["bg"])
    in_specs = [pl.BlockSpec((1, tm, d), lambda i, j: (i, j, 0)),
                pl.BlockSpec((1, 6, d), (lambda i, j: (i, 0, 0)) if per_batch_mod else (lambda i, j: (0, 0, 0))),
                _const_spec(ng.shape)]
    in_specs += [_const_spec(w.shape) for w in weights]
    args = [x, mod, ng, *weights]
    if rotate:
        in_specs += [pl.BlockSpec((tm, t.shape[1]), lambda i, j: (j, 0)) for t in rope_tabs]
        args += list(rope_tabs)
    out_specs, out_shape = [], []
    for name, w, dt in _PROJ_OUT:
        if name in _HEAD_MAJOR:
            out_specs.append(pl.BlockSpec((1, N_H, tm, w), lambda i, j: (i, 0, j, 0)))
            out_shape.append(jax.ShapeDtypeStruct((b, N_H, n, w), dt))
        else:
            out_specs.append(pl.BlockSpec((1, tm, w), lambda i, j: (i, j, 0)))
            out_shape.append(jax.ShapeDtypeStruct((b, n, w), dt))
    outs = pl.pallas_call(
        functools.partial(_proj_kernel, rotate=rotate),
        grid=(b, n // tm),
        in_specs=in_specs,
        out_specs=out_specs,
        out_shape=out_shape,
        compiler_params=_params("parallel", "parallel"),
        name="in_proj_rot" if rotate else "in_proj",
    )(*args)
    return {name: o for (name, _, _), o in zip(_PROJ_OUT, outs)}


def _attn_kernel(*refs, diff, has_lat, tk, lambda_init):
    q_ref, kc_ref, vc_ref = refs[:3]
    refs = refs[3:]
    if has_lat:
        kl_ref, vl_ref = refs[:2]
        refs = refs[2:]
    if diff:
        lam_ref, g_ref = refs[:2]
        refs = refs[2:]
    if diff:
        o_ref, acc_ref, qst_ref = refs
    else:
        o_ref, acc_ref = refs
    tq = o_ref.shape[1]
    n_maps = 2 if diff else 1

    if diff:
        q = q_ref[0]
        lane_q = _iota((1, q.shape[-1]), 1)
        for i in range(2 * N_H):
            qst_ref[i * tq:(i + 1) * tq, :] = jnp.where((lane_q // DIFF_QK) == i, q, jnp.zeros_like(q))

    def scores(k_ref, rows):
        if diff:
            return [(_dot_nt(qst_ref[...], k_ref[0, rows, :]), 0)]
        return [(_dot_nt(q_ref[0, h], k_ref[0, h, rows, :]), h * tq) for h in range(N_H)]

    def step(k_ref, v_ref, rows, ms):
        new = []
        for g, (s, r0) in enumerate(scores(k_ref, rows)):
            mx = jnp.max(s, axis=-1, keepdims=True)
            if ms is not None:
                mx = jnp.maximum(ms[g], mx)
                alpha = jnp.exp2(ms[g] - mx)
            p = jnp.exp2(s - mx).astype(BF16)
            hrows = n_maps * tq
            for j in range(s.shape[0] // hrows):
                h = (r0 + j * hrows) // hrows
                pv = _dot(p[j * hrows:(j + 1) * hrows], v_ref[0, h, rows, :])
                arows = slice(r0 + j * hrows, r0 + (j + 1) * hrows)
                if ms is None:
                    acc_ref[arows, :] = pv
                else:
                    acc_ref[arows, :] = alpha[j * hrows:(j + 1) * hrows] * acc_ref[arows, :] + pv
            new.append(mx)
        return tuple(new)

    ms = step(kc_ref, vc_ref, slice(None), None)
    if has_lat:
        def body(c, carry):
            return step(kl_ref, vl_ref, pl.ds(pl.multiple_of(c * tk, tk), tk), carry)

        lax.fori_loop(0, kl_ref.shape[-2] // tk, body, ms)

    def head_out(i):
        a = acc_ref[i * tq:(i + 1) * tq, :]
        return a[:, :HEAD_DIM] * (1.0 / a[:, V_ONES:V_ONES + 1])

    if diff:
        lv = lam_ref[...]
        lam = (jnp.exp(jnp.sum(lv[0:1] * lv[1:2], axis=-1, keepdims=True))
               - jnp.exp(jnp.sum(lv[2:3] * lv[3:4], axis=-1, keepdims=True)) + lambda_init)
        out = jnp.concatenate([head_out(2 * h) - lam * head_out(2 * h + 1) for h in range(N_H)], axis=1)
        out = _group_norm(out, g_ref[...], _group_mean_matrix(GROUP_W, HEAD_DIM)) * (1.0 - lambda_init)
    else:
        out = jnp.concatenate([head_out(h) for h in range(N_H)], axis=1)
    o_ref[0] = out.astype(o_ref.dtype)


def _attn_call(q, kc, vc, kl, vl, diff_args, tq, tk, name):
    has_lat = kl is not None
    diff = diff_args is not None
    b, lq = q.shape[0], q.shape[-2]

    def spec(arr, rows, tiled):
        if arr.ndim == 4:
            return pl.BlockSpec((1, N_H, rows, arr.shape[3]), (lambda i, j: (i, 0, j, 0)) if tiled else (lambda i, j: (i, 0, 0, 0)))
        return pl.BlockSpec((1, rows, arr.shape[2]), (lambda i, j: (i, j, 0)) if tiled else (lambda i, j: (i, 0, 0)))

    in_specs = [spec(q, tq, True), spec(kc, kc.shape[-2], False), spec(vc, vc.shape[-2], False)]
    args = [q, kc, vc]
    if has_lat:
        in_specs += [spec(kl, kl.shape[-2], False), spec(vl, vl.shape[-2], False)]
        args += [kl, vl]
    lambda_init = 0.0
    if diff:
        lam_vecs, sub_g, lambda_init = diff_args
        in_specs += [_const_spec(lam_vecs.shape), _const_spec(sub_g.shape)]
        args += [lam_vecs, sub_g]
    return pl.pallas_call(
        functools.partial(_attn_kernel, diff=diff, has_lat=has_lat, tk=tk, lambda_init=lambda_init),
        grid=(b, lq // tq),
        in_specs=in_specs,
        out_specs=pl.BlockSpec((1, tq, GROUP_W), lambda i, j: (i, j, 0)),
        out_shape=jax.ShapeDtypeStruct((b, lq, GROUP_W), BF16),
        scratch_shapes=[pltpu.VMEM((N_H * (2 if diff else 1) * tq, V_SLOT), F32)]
                       + ([pltpu.VMEM((2 * N_H * tq, q.shape[-1]), BF16)] if diff else []),
        compiler_params=_params("parallel", "parallel"),
        name=name,
    )(*args)


def _gla_chunks(work, consts):
    tri_f, tri_b, rexp, bdmask, rowid = consts
    c, sb = GLA_CHUNK, GLA_SUB
    qk_tiles, tiles_per_sb, n_sb = c // 8, sb // 8, c // sb
    lane_k = _iota((1, 128), 1)
    lane_o = _iota((1, GROUP_W), 1)

    bs = [_dot(tri_b if rev else tri_f, a, precision=HIGHEST) * LOG2E for (_, _, _, a, _, rev) in work]

    e_alls, spans_all, qms, kss, q_ins, k_ends, b_ends = [], [], [], [], [], [], []
    for (q, k, v, a, st, rev), b in zip(work, bs):
        pieces, spans = [], []
        for j in range(c):
            t, blk = j // 8, j // sb
            t0, t1 = (blk * tiles_per_sb, t + 1) if rev else (t, (blk + 1) * tiles_per_sb)
            for tt in range(t0, t1):
                rows = slice(8 * tt, 8 * tt + 8)
                arg = b[rows] - b[j:j + 1]
                if tt == t:
                    keep = (rowid[rows] <= j) if rev else (rowid[rows] >= j)
                    arg = jnp.where(keep, arg, NEG_BIG)
                pieces.append(jnp.exp2(arg) * (q[rows] * k[j:j + 1]))
            spans.append((t0, t1))
        e_alls.append(jnp.concatenate(pieces, axis=0).astype(BF16))
        spans_all.append(spans)
        qm_d, ks_d = [], []
        for blk in (range(n_sb - 1) if rev else range(1, n_sb)):
            r0 = blk * sb
            if rev:
                ref_row, seen = b[r0 + sb:r0 + sb + 1], rowid >= r0 + sb
            else:
                ref_row, seen = b[r0 - 1:r0], rowid < r0
            qs = q[r0:r0 + sb] * jnp.exp2(b[r0:r0 + sb] - ref_row)
            ks_d.append((k * jnp.exp2(jnp.where(seen, ref_row - b, NEG_BIG))).astype(BF16))
            qm_d.append(jnp.concatenate([jnp.where((lane_k // GLA_DK) == h, qs, 0.0) for h in range(N_H)],
                                        axis=0).astype(BF16))
        qms.append(qm_d)
        kss.append(ks_d)
        b_end = b[0:1] if rev else b[c - 1:c]
        b_ends.append(b_end)
        q_ins.append((q * jnp.exp2(b)).astype(BF16))
        k_ends.append((k * jnp.exp2(b_end - b)).astype(BF16))

    sxs = [_dot(e_all, rexp) for e_all in e_alls]
    s_ods = [jnp.concatenate([_dot_nt(qm, ks) for qm, ks in zip(qm_d, ks_d)], axis=0).astype(BF16)
             for qm_d, ks_d in zip(qms, kss)]
    o_ins = [_dot_nt(q_in, w[4].astype(BF16)) for q_in, w in zip(q_ins, work)]
    upds = [_dot_tn(w[2].astype(BF16), k_end) for k_end, w in zip(k_ends, work)]

    ress = [_dot(s_od, w[2].astype(BF16)) for s_od, w in zip(s_ods, work)]

    outs = []
    for d, (q, k, v, a, st, rev) in enumerate(work):
        acc = [jnp.zeros((8, GROUP_W), F32) for _ in range(qk_tiles)]
        v32 = v.astype(F32)
        off = 0
        for j in range(c):
            t0, t1 = spans_all[d][j]
            vj = v32[j:j + 1]
            for t in range(t0, t1):
                acc[t] = acc[t] + sxs[d][off:off + 8] * vj
                off += 8
        for n, blk in enumerate(range(n_sb - 1) if rev else range(1, n_sb)):
            o_blk = jnp.zeros((sb, GROUP_W), F32)
            for h in range(N_H):
                r0 = (n * N_H + h) * sb
                o_blk = o_blk + jnp.where((lane_o // HEAD_DIM) == h, ress[d][r0:r0 + sb], 0.0)
            for tt in range(tiles_per_sb):
                t = blk * tiles_per_sb + tt
                acc[t] = acc[t] + o_blk[8 * tt:8 * tt + 8]
        o = jnp.concatenate(acc, axis=0) + o_ins[d]
        outs.append((o, st * jnp.exp2(b_ends[d]) + jnp.where(bdmask, upds[d], 0.0)))
    return outs


def _gla_kernel(qc_ref, kc_ref, vc_ref, afc_ref, abc_ref, ql_ref, kl_ref, vl_ref, afl_ref, abl_ref,
                oc_ref, ol_ref, sf_ref, sb_ref):
    c = GLA_CHUNK
    ri, ci = _iota((c, c), 0), _iota((c, c), 1)
    tri_f = jnp.where(ri >= ci, 1.0, 0.0).astype(F32)
    tri_b = jnp.where(ri <= ci, 1.0, 0.0).astype(F32)
    rexp = jnp.where((_iota((128, GROUP_W), 0) // GLA_DK) == (_iota((128, GROUP_W), 1) // HEAD_DIM),
                     1.0, 0.0).astype(BF16)
    bdmask = (_iota((GROUP_W, 128), 0) // HEAD_DIM) == (_iota((GROUP_W, 128), 1) // GLA_DK)
    rowid = _iota((c, 1), 0)
    consts = (tri_f, tri_b, rexp, bdmask, rowid)

    sf_ref[...] = jnp.zeros_like(sf_ref)
    sb_ref[...] = jnp.zeros_like(sb_ref)
    oc_ref[...] = jnp.zeros_like(oc_ref)
    ol_ref[...] = jnp.zeros_like(ol_ref)

    def sweep(q_ref, k_ref, v_ref, af_ref, ab_ref, o_ref):
        n = q_ref.shape[1] // c

        def body(i, carry):
            rows_f = pl.ds(pl.multiple_of(i * c, c), c)
            rows_b = pl.ds(pl.multiple_of((n - 1 - i) * c, c), c)
            work = [(q_ref[0, rows, :], k_ref[0, rows, :], v_ref[0, rows, :], a_ref[0, rows, :], st_ref[...], rev)
                    for rows, a_ref, st_ref, rev in ((rows_f, af_ref, sf_ref, False), (rows_b, ab_ref, sb_ref, True))]
            (o_f, st_f), (o_b, st_b) = _gla_chunks(work, consts)
            sf_ref[...] = st_f
            sb_ref[...] = st_b
            o_ref[0, rows_f, :] += o_f
            o_ref[0, rows_b, :] += o_b
            return carry

        lax.fori_loop(0, n, body, 0)

    sweep(qc_ref, kc_ref, vc_ref, afc_ref, abc_ref, oc_ref)
    sweep(ql_ref, kl_ref, vl_ref, afl_ref, abl_ref, ol_ref)


def _seq_spec(arr):
    return pl.BlockSpec((1,) + arr.shape[1:], lambda i: (i, 0, 0))


def _gla_call(pc, plat):
    b = pc["gq"].shape[0]
    args = [pc["gq"], pc["gk"], pc["gv"], pc["gaf"], pc["gab"],
            plat["gq"], plat["gk"], plat["gv"], plat["gaf"], plat["gab"]]
    lc, ll = pc["gq"].shape[1], plat["gq"].shape[1]
    return pl.pallas_call(
        _gla_kernel,
        grid=(b,),
        in_specs=[_seq_spec(a) for a in args],
        out_specs=[pl.BlockSpec((1, lc, GROUP_W), lambda i: (i, 0, 0)),
                   pl.BlockSpec((1, ll, GROUP_W), lambda i: (i, 0, 0))],
        out_shape=[jax.ShapeDtypeStruct((b, lc, GROUP_W), F32), jax.ShapeDtypeStruct((b, ll, GROUP_W), F32)],
        scratch_shapes=[pltpu.VMEM((GROUP_W, 128), F32), pltpu.VMEM((GROUP_W, 128), F32)],
        compiler_params=_params("parallel"),
        name="gla_scan",
    )(*args)


def _ret_kernel(lg_ref, qc_ref, kc_ref, vc_ref, ql_ref, kl_ref, vl_ref, oc_ref, ol_ref,
                sf_ref, sb_ref, dm_ref):
    c = RET_CHUNK
    lgam = _log_sigmoid(lg_ref[...])
    ri, ci = _iota((c, c), 0), _iota((c, c), 1)
    pos = _iota((c, 128), 0).astype(F32)
    dq, dk, dc = [], [], []
    for d in range(2):
        lg = lgam[d:d + 1]
        dist = (ri - ci) if d == 0 else (ci - ri)
        for h in range(N_H):
            lg_h = lg[:, h * RET_DK:h * RET_DK + 1]
            dm_ref[d, h] = jnp.exp(jnp.where(dist >= 0, dist.astype(F32) * lg_h, NEG_BIG))
        dq.append(jnp.exp(((pos + 1.0) if d == 0 else (c - pos)) * lg))
        dk.append(jnp.exp(((c - 1.0 - pos) if d == 0 else pos) * lg))
        dc.append(jnp.exp(c * lg))
    bdmask = (_iota((GROUP_W, 128), 0) // HEAD_DIM) == (_iota((GROUP_W, 128), 1) // RET_DK)
    lane_q = _iota((1, 128), 1)
    lane_o = _iota((1, GROUP_W), 1)

    sf_ref[...] = jnp.zeros_like(sf_ref)
    sb_ref[...] = jnp.zeros_like(sb_ref)
    oc_ref[...] = jnp.zeros_like(oc_ref)
    ol_ref[...] = jnp.zeros_like(ol_ref)

    def chunk(q, k, v, st_ref, d):
        st = st_ref[...]
        k16, v16 = k.astype(BF16), v.astype(BF16)
        o = _dot_nt((q * dq[d]).astype(BF16), st.astype(BF16))
        for h in range(N_H):
            qh = jnp.where((lane_q // RET_DK) == h, q, 0.0).astype(BF16)
            s = _dot_nt(qh, k16) * dm_ref[d, h]
            o = o + jnp.where((lane_o // HEAD_DIM) == h, _dot(s.astype(BF16), v16), 0.0)
        upd = _dot_tn(v16, (k * dk[d]).astype(BF16))
        st_ref[...] = st * dc[d] + jnp.where(bdmask, upd, 0.0)
        return o

    def sweep(q_ref, k_ref, v_ref, o_ref):
        n = q_ref.shape[1] // c

        def body(i, carry):
            for d, idx, st_ref in ((0, i, sf_ref), (1, n - 1 - i, sb_ref)):
                rows = pl.ds(pl.multiple_of(idx * c, c), c)
                o_ref[0, rows, :] += chunk(q_ref[0, rows, :], k_ref[0, rows, :], v_ref[0, rows, :], st_ref, d)
            return carry

        lax.fori_loop(0, n, body, 0)

    sweep(qc_ref, kc_ref, vc_ref, oc_ref)
    sweep(ql_ref, kl_ref, vl_ref, ol_ref)


def _ret_call(lg, pc, plat):
    b = pc["rq"].shape[0]
    args = [pc["rq"], pc["rk"], pc["rv"], plat["rq"], plat["rk"], plat["rv"]]
    lc, ll = pc["rq"].shape[1], plat["rq"].shape[1]
    return pl.pallas_call(
        _ret_kernel,
        grid=(b,),
        in_specs=[_const_spec(lg.shape)] + [_seq_spec(a) for a in args],
        out_specs=[pl.BlockSpec((1, lc, GROUP_W), lambda i: (i, 0, 0)),
                   pl.BlockSpec((1, ll, GROUP_W), lambda i: (i, 0, 0))],
        out_shape=[jax.ShapeDtypeStruct((b, lc, GROUP_W), F32), jax.ShapeDtypeStruct((b, ll, GROUP_W), F32)],
        scratch_shapes=[pltpu.VMEM((GROUP_W, 128), F32), pltpu.VMEM((GROUP_W, 128), F32),
                        pltpu.VMEM((2, N_H, RET_CHUNK, RET_CHUNK), F32)],
        compiler_params=_params("parallel"),
        name="ret_scan",
    )(lg, *args)


def _mix_out_kernel(x_ref, mod_ref, ng_ref, do_ref, mo_ref, go_ref, gr_ref, ro_ref, rg_ref,
                    gg_ref, rgn_ref, wo_ref, x1_ref, h2_ref):
    gmat = _group_mean_matrix(GROUP_W, HEAD_DIM)
    gl = _group_norm(go_ref[0], gg_ref[...], gmat) * _silu(gr_ref[0].astype(F32))
    rt = _group_norm(ro_ref[0], rgn_ref[...], gmat) * _silu(rg_ref[0].astype(F32))
    ml = (_dot(do_ref[0], wo_ref[0:256]) + _dot(mo_ref[0], wo_ref[256:512])
          + _dot(gl.astype(BF16), wo_ref[512:768]) + _dot(rt.astype(BF16), wo_ref[768:1024]))
    x1 = x_ref[0] + mod_ref[0, 2:3] * _rms(ml, ng_ref[1:2])
    x1_ref[0] = x1
    h2_ref[0] = (_rms(x1, ng_ref[2:3]) * (1.0 + mod_ref[0, 4:5]) + mod_ref[0, 3:4]).astype(BF16)


def _mod_spec(mod, d):
    if mod.shape[0] > 1:
        return pl.BlockSpec((1, 6, d), lambda i, j: (i, 0, 0))
    return pl.BlockSpec((1, 6, d), lambda i, j: (0, 0, 0))


def _mix_out_call(x, mod, ng, d_o, m_o, g_o, g_r, r_o, r_g, lw, tm):
    b, n, d = x.shape

    def tok(w):
        return pl.BlockSpec((1, tm, w), lambda i, j: (i, j, 0))

    return pl.pallas_call(
        _mix_out_kernel,
        grid=(b, n // tm),
        in_specs=[tok(d), _mod_spec(mod, d), _const_spec(ng.shape)] + [tok(GROUP_W)] * 6
                 + [_const_spec(lw["gla_g"].shape), _const_spec(lw["ret_g"].shape), _const_spec(lw["w_out"].shape)],
        out_specs=[tok(d), tok(d)],
        out_shape=[jax.ShapeDtypeStruct((b, n, d), F32), jax.ShapeDtypeStruct((b, n, d), BF16)],
        compiler_params=_params("parallel", "parallel"),
        name="mix_out",
    )(x, mod, ng, d_o, m_o, g_o, g_r, r_o, r_g, lw["gla_g"], lw["ret_g"], lw["w_out"])


def _ffn_kernel(h_ref, x1_ref, mod_ref, ng_ref, wg_ref, wu_ref, wo_ref, x2_ref):
    h = h_ref[0]
    a = (_silu(_dot(h, wg_ref[...])) * _dot(h, wu_ref[...])).astype(BF16)
    y = _dot(a, wo_ref[...])
    x2_ref[0] = x1_ref[0] + mod_ref[0, 5:6] * _rms(y, ng_ref[3:4])


def _ffn_call(h2, x1, mod, ng, lw, tm):
    b, n, d = x1.shape

    def tok(w):
        return pl.BlockSpec((1, tm, w), lambda i, j: (i, j, 0))

    return pl.pallas_call(
        _ffn_kernel,
        grid=(b, n // tm),
        in_specs=[tok(d), tok(d), _mod_spec(mod, d), _const_spec(ng.shape),
                  _const_spec(lw["ffn_g"].shape), _const_spec(lw["ffn_u"].shape), _const_spec(lw["ffn_o"].shape)],
        out_specs=tok(d),
        out_shape=jax.ShapeDtypeStruct((b, n, d), F32),
        compiler_params=_params("parallel", "parallel"),
        name="ffn",
    )(h2, x1, mod, ng, lw["ffn_g"], lw["ffn_u"], lw["ffn_o"])


def _pack_layer(l, w_in, diff_norm, mla_q_norm, mla_kv_norm, mla_w_uq, mla_w_ukv, gla_w_gate, gla_b_gate,
                gla_norm, ret_decay, ret_norm, w_out, ffn_w_in, ffn_w_out):
    d = w_in.shape[1]
    w = w_in[l]
    sizes = (256, 256, 256, 256, 128, 32, 128, 128, 256, 256, 16, 16, 128, 128, 256, 256)
    offs = np.concatenate([[0], np.cumsum(sizes)])
    seg = [w[:, offs[i]:offs[i + 1]] for i in range(len(sizes))]
    z = lambda n: jnp.zeros((d, n), w.dtype)
    misc = jnp.concatenate([seg[10], seg[11], z(32), seg[5], z(32)], axis=1)
    wm = jnp.concatenate(seg[0:5] + [misc] + seg[6:10] + seg[12:16], axis=1).astype(BF16)

    uq = mla_w_uq[l].reshape(-1, N_H, MLA_NOPE + MLA_ROPE)
    uq = jnp.pad(uq, ((0, 0), (0, 0), (0, MLA_SLOT - MLA_NOPE - MLA_ROPE))).reshape(-1, N_H * MLA_SLOT)
    ukv = mla_w_ukv[l].reshape(-1, N_H, MLA_NOPE + HEAD_DIM)
    uk = jnp.pad(ukv[:, :, :MLA_NOPE], ((0, 0), (0, 0), (0, MLA_SLOT - MLA_NOPE))).reshape(-1, N_H * MLA_SLOT)
    uv = ukv[:, :, MLA_NOPE:].reshape(-1, GROUP_W)

    wgf = jnp.zeros((128, 128), F32).at[0:GLA_GATE_RANK].set(gla_w_gate[l, 0]).astype(BF16)
    wgb = jnp.zeros((128, 128), F32).at[GLA_GATE_RANK:2 * GLA_GATE_RANK].set(gla_w_gate[l, 1]).astype(BF16)
    hid = ffn_w_out.shape[1]
    return {
        "wm": wm, "qg": mla_q_norm[l][None], "kvg": mla_kv_norm[l][None],
        "wuq": uq.astype(BF16), "wuk": uk.astype(BF16), "wuv": uv.astype(BF16),
        "wgf": wgf, "wgb": wgb, "bg": gla_b_gate[l],
        "diff_g": jnp.tile(diff_norm[l], N_H)[None], "gla_g": jnp.tile(gla_norm[l], N_H)[None],
        "ret_g": jnp.tile(ret_norm[l], N_H)[None],
        "ret_lg": jnp.repeat(ret_decay[l], RET_DK, axis=1),
        "w_out": w_out[l].astype(BF16),
        "ffn_g": ffn_w_in[l][:, :hid].astype(BF16), "ffn_u": ffn_w_in[l][:, hid:].astype(BF16),
        "ffn_o": ffn_w_out[l].astype(BF16),
    }


def _rope_tables(seq):
    rows = seq // GRID_W
    row = jnp.repeat(jnp.arange(rows, dtype=F32), GRID_W)
    col = jnp.tile(jnp.arange(GRID_W, dtype=F32), rows)
    n_freq = ROT_DIM // 4
    freqs = ROPE_THETA ** (-jnp.arange(n_freq, dtype=F32) / n_freq)
    ang = jnp.concatenate([row[:, None] * freqs, col[:, None] * freqs], axis=-1)
    cos, sin = jnp.cos(ang), jnp.sin(ang)
    c32 = jnp.concatenate([cos, cos], axis=-1)
    s32 = jnp.concatenate([-sin, sin], axis=-1)
    one, zero = jnp.ones((seq, 1), F32), jnp.zeros((seq, 1), F32)
    slot_c = jnp.concatenate([jnp.tile(one, (1, MLA_NOPE)), c32, jnp.tile(one, (1, MLA_SLOT - MLA_NOPE - MLA_ROPE))], -1)
    slot_s = jnp.concatenate([jnp.tile(zero, (1, MLA_NOPE)), s32, jnp.tile(zero, (1, MLA_SLOT - MLA_NOPE - MLA_ROPE))], -1)
    return (jnp.tile(c32, (1, 256 // ROT_DIM)), jnp.tile(s32, (1, 256 // ROT_DIM)),
            jnp.tile(slot_c, (1, N_H)), jnp.tile(slot_s, (1, N_H)))


def _tile(n, want):
    t = min(n, want)
    while n % t:
        t //= 2
    return t


def kernel(x, c, ctx, c_ctx, ada_w, ada_b, norm_g, w_in, diff_lam, diff_norm, mla_q_norm, mla_kv_norm,
           mla_w_uq, mla_w_ukv, gla_w_gate, gla_b_gate, gla_norm, ret_decay, ret_norm, w_out, ffn_w_in, ffn_w_out):
    b, seq, d = x.shape
    lc = ctx.shape[1]
    depth = ada_w.shape[0]
    rope_tabs = _rope_tables(seq)

    pad = (-(b + 1)) % 8
    s_all = jnp.concatenate([c, c_ctx[None], jnp.zeros((pad, d), F32)], axis=0)
    mods = _ada_call(s_all, ada_w, ada_b)

    tm_l, tm_c = _tile(seq, 512), _tile(lc, 256)
    tq_c = _tile(lc, 256)
    tq_d, tk_d = _tile(seq, 256), _tile(seq, 1024)
    tq_m, tk_m = _tile(seq, 512), _tile(seq, 1024)

    xl, xc = x, ctx
    for l in range(depth):
        with_ctx = l < depth - 1
        lambda_init = 0.8 - 0.6 * math.exp(-0.3 * l)
        lw = _pack_layer(l, w_in, diff_norm, mla_q_norm, mla_kv_norm, mla_w_uq, mla_w_ukv, gla_w_gate,
                         gla_b_gate, gla_norm, ret_decay, ret_norm, w_out, ffn_w_in, ffn_w_out)
        mod_l = mods[l, :b].reshape(b, 6, d)
        mod_c = mods[l, b:b + 1].reshape(1, 6, d)
        ng = norm_g[l]

        pc = _proj_call(xc, mod_c, ng, lw, None, tm_c)
        pt = _proj_call(xl, mod_l, ng, lw, rope_tabs, tm_l)

        dargs = (diff_lam[l], lw["diff_g"], lambda_init)
        d_l = _attn_call(pt["dq"], pc["dk"], pc["dv"], pt["dk"], pt["dv"], dargs, tq_d, tk_d, "diff_attn")
        m_l = _attn_call(pt["mq"], pc["mk"], pc["mv"], pt["mk"], pt["mv"], None, tq_m, tk_m, "mla_attn")
        g_c, g_l = _gla_call(pc, pt)
        r_c, r_l = _ret_call(lw["ret_lg"], pc, pt)

        x1, h2 = _mix_out_call(xl, mod_l, ng, d_l, m_l, g_l, pt["gr"], r_l, pt["rg"], lw, tm_l)
        xl = _ffn_call(h2, x1, mod_l, ng, lw, tm_l)
        if with_ctx:
            d_c = _attn_call(pc["dq"], pc["dk"], pc["dv"], None, None, dargs, tq_c, tk_d, "diff_attn_ctx")
            m_c = _attn_call(pc["mq"], pc["mk"], pc["mv"], None, None, None, tq_c, tk_m, "mla_attn_ctx")
            x1c, h2c = _mix_out_call(xc, mod_c, ng, d_c, m_c, g_c, pc["gr"], r_c, pc["rg"], lw, tm_c)
            xc = _ffn_call(h2c, x1c, mod_c, ng, lw, tm_c)
    return xl
```

```python
import functools
import math

import jax
import jax.numpy as jnp
import numpy as np
from jax import lax
from jax.experimental import pallas as pl
from jax.experimental.pallas import tpu as pltpu

GRID_W = 64
N_H = 4
HEAD_DIM = 64
GROUP_W = N_H * HEAD_DIM
ROT_DIM = 32
ROPE_THETA = 10000.0
DIFF_QK = 32
MLA_NOPE = 64
MLA_ROPE = 32
MLA_SLOT = 128
V_SLOT = 128
V_ONES = HEAD_DIM
LOG2E = math.log2(math.e)
GLA_DK = 32
GLA_GATE_RANK = 16
GLA_GATE_NORM = 16.0
GLA_CHUNK = 64
GLA_SUB = 16
RET_DK = 32
RET_CHUNK = 256
RMS_EPS = 1e-6
NEG_BIG = -1e30

VMEM_LIMIT_BYTES = 48 * 1024 * 1024
F32 = jnp.float32
BF16 = jnp.bfloat16
HIGHEST = lax.Precision.HIGHEST

C_DQ, C_DK, C_DV, C_CQ, C_CKV, C_MISC = 0, 256, 512, 768, 1024, 1152
C_GQ, C_GK, C_GV, C_GR, C_RQ, C_RK, C_RV, C_RG, C_END = 1280, 1408, 1536, 1792, 2048, 2176, 2304, 2560, 2816
MISC_KR = 64


def _dot(a, b, precision=None):
    return jnp.dot(a, b, preferred_element_type=F32, precision=precision)


def _dot_nt(a, b):
    return lax.dot_general(a, b, (((1,), (1,)), ((), ())), preferred_element_type=F32)


def _dot_tn(a, b):
    return lax.dot_general(a, b, (((0,), (0,)), ((), ())), preferred_element_type=F32)


def _rms(x, g):
    ms = jnp.mean(x * x, axis=-1, keepdims=True)
    return x * lax.rsqrt(ms + RMS_EPS) * g


def _silu(x):
    return x * (1.0 / (1.0 + jnp.exp(-x)))


def _log_sigmoid(x):
    return jnp.minimum(x, 0.0) - jnp.log1p(jnp.exp(-jnp.abs(x)))


def _iota(shape, dim):
    return lax.broadcasted_iota(jnp.int32, shape, dim)


def _group_mean_matrix(width, group):
    same = (_iota((width, width), 0) // group) == (_iota((width, width), 1) // group)
    return jnp.where(same, 1.0 / group, 0.0).astype(BF16)


def _group_norm(o, g, gmat):
    ms = _dot((o * o).astype(BF16), gmat)
    return o * lax.rsqrt(ms + RMS_EPS) * g


def _rope(t, cs, sn):
    w = t.shape[-1]
    lo = (_iota((1, w), 1) % ROT_DIM) < (ROT_DIM // 2)
    swapped = jnp.where(lo, pltpu.roll(t, w - ROT_DIM // 2, 1), pltpu.roll(t, ROT_DIM // 2, 1))
    return t * cs + swapped * sn


def _const_spec(shape):
    nd = len(shape)
    return pl.BlockSpec(shape, lambda *_: (0,) * nd)


def _params(*sem):
    return pltpu.CompilerParams(dimension_semantics=sem, vmem_limit_bytes=VMEM_LIMIT_BYTES)


def _ada_kernel(s_ref, w_ref, b_ref, o_ref):
    s = _silu(s_ref[...])
    o_ref[0] = _dot(s, w_ref[0], precision=HIGHEST) + b_ref[0]


def _ada_call(s_all, ada_w, ada_b):
    depth, d, n = ada_w.shape
    rows = s_all.shape[0]
    tn = 1024
    return pl.pallas_call(
        _ada_kernel,
        grid=(depth, n // tn),
        in_specs=[pl.BlockSpec((rows, d), lambda l, j: (0, 0)),
                  pl.BlockSpec((1, d, tn), lambda l, j: (l, 0, j)),
                  pl.BlockSpec((1, 1, tn), lambda l, j: (l, 0, j))],
        out_specs=pl.BlockSpec((1, rows, tn), lambda l, j: (l, 0, j)),
        out_shape=jax.ShapeDtypeStruct((depth, rows, n), F32),
        compiler_params=_params("parallel", "parallel"),
        name="ada_mod",
    )(s_all, ada_w, ada_b.reshape(depth, 1, n))


def _proj_kernel(*refs, rotate):
    (x_ref, mod_ref, ng_ref, wm_ref, qg_ref, kvg_ref, wuq_ref, wuk_ref, wuv_ref,
     wgf_ref, wgb_ref, bg_ref) = refs[:12]
    refs = refs[12:]
    if rotate:
        cs256_ref, sn256_ref, cs512_ref, sn512_ref = refs[:4]
        refs = refs[4:]
    (dq_ref, dk_ref, dv_ref, mq_ref, mk_ref, mv_ref, gq_ref, gk_ref, gv_ref, gr_ref,
     gaf_ref, gab_ref, rq_ref, rk_ref, rv_ref, rg_ref) = refs

    x = x_ref[0]
    h = (_rms(x, ng_ref[0:1]) * (1.0 + mod_ref[0, 1:2]) + mod_ref[0, 0:1]).astype(BF16)

    def proj(lo, hi):
        return _dot(h, wm_ref[:, lo:hi])

    def rope256(t):
        return _rope(t, cs256_ref[...], sn256_ref[...]) if rotate else t

    def rope128(t):
        return _rope(t, cs256_ref[:, :128], sn256_ref[:, :128]) if rotate else t

    def rope512(t):
        return _rope(t, cs512_ref[...], sn512_ref[...]) if rotate else t

    lane = _iota((1, 128), 1)
    ones_lane = jnp.where(lane == V_ONES, 1.0, 0.0)

    def store_values(v_ref, v):
        rolled = pltpu.roll(v, GROUP_W - HEAD_DIM, 1)
        for hd in range(N_H):
            src = v if hd % 2 == 0 else rolled
            blk = src[:, (hd // 2) * 128:(hd // 2 + 1) * 128]
            v_ref[0, hd] = jnp.where(lane < HEAD_DIM, blk, ones_lane).astype(BF16)

    dq_ref[0] = (rope256(proj(C_DQ, C_DK)) * (DIFF_QK ** -0.5 * LOG2E)).astype(BF16)
    dk_ref[0] = rope256(proj(C_DK, C_DV)).astype(BF16)
    store_values(dv_ref, proj(C_DV, C_CQ))

    qn = _rms(proj(C_CQ, C_CKV), qg_ref[...]).astype(BF16)
    q512 = (rope512(_dot(qn, wuq_ref[...])) * ((MLA_NOPE + MLA_ROPE) ** -0.5 * LOG2E)).astype(BF16)
    kvn = _rms(proj(C_CKV, C_MISC), kvg_ref[...]).astype(BF16)
    misc = proj(C_MISC, C_GQ)
    kr = jnp.where((lane >= MISC_KR) & (lane < MISC_KR + MLA_ROPE), misc, 0.0)
    k512 = rope512(_dot(kvn, wuk_ref[...]) + jnp.concatenate([kr] * N_H, axis=1)).astype(BF16)
    for hd in range(N_H):
        mq_ref[0, hd] = q512[:, hd * MLA_SLOT:(hd + 1) * MLA_SLOT]
        mk_ref[0, hd] = k512[:, hd * MLA_SLOT:(hd + 1) * MLA_SLOT]
    store_values(mv_ref, _dot(kvn, wuv_ref[...]))

    gq_ref[0] = proj(C_GQ, C_GK) * GLA_DK ** -0.5
    gk_ref[0] = proj(C_GK, C_GV)
    gv_ref[0] = proj(C_GV, C_GR).astype(BF16)
    gr_ref[0] = proj(C_GR, C_RQ).astype(BF16)
    misc16 = misc.astype(BF16)
    gaf_ref[0] = _log_sigmoid(_dot(misc16, wgf_ref[...]) + bg_ref[0:1]) / GLA_GATE_NORM
    gab_ref[0] = _log_sigmoid(_dot(misc16, wgb_ref[...]) + bg_ref[1:2]) / GLA_GATE_NORM

    rq_ref[0] = rope128(proj(C_RQ, C_RK))
    rk_ref[0] = rope128(proj(C_RK, C_RV) * RET_DK ** -0.5)
    rv_ref[0] = proj(C_RV, C_RG).astype(BF16)
    rg_ref[0] = proj(C_RG, C_END).astype(BF16)


_HEAD_MAJOR = ("mq", "mk", "dv", "mv")
_PROJ_OUT = (("dq", 256, BF16), ("dk", 256, BF16), ("dv", V_SLOT, BF16),
             ("mq", MLA_SLOT, BF16), ("mk", MLA_SLOT, BF16), ("mv", V_SLOT, BF16),
             ("gq", 128, F32), ("gk", 128, F32), ("gv", 256, BF16), ("gr", 256, BF16),
             ("gaf", 128, F32), ("gab", 128, F32),
             ("rq", 128, F32), ("rk", 128, F32), ("rv", 256, BF16), ("rg", 256, BF16))


def _proj_call(x, mod, ng, lw, rope_tabs, tm):
    b, n, d = x.shape
    rotate = rope_tabs is not None
    per_batch_mod = mod.shape[0] > 1
    weights = (lw["wm"], lw["qg"], lw["kvg"], lw["wuq"], lw["wuk"], lw["wuv"], lw["wgf"], lw["wgb"], lw["bg"])
    in_specs = [pl.BlockSpec((1, tm, d), lambda i, j: (i, j, 0)),
                pl.BlockSpec((1, 6, d), (lambda i, j: (i, 0, 0)) if per_batch_mod else (lambda i, j: (0, 0, 0))),
                _const_spec(ng.shape)]
    in_specs += [_const_spec(w.shape) for w in weights]
    args = [x, mod, ng, *weights]
    if rotate:
        in_specs += [pl.BlockSpec((tm, t.shape[1]), lambda i, j: (j, 0)) for t in rope_tabs]
        args += list(rope_tabs)
    out_specs, out_shape = [], []
    for name, w, dt in _PROJ_OUT:
        if name in _HEAD_MAJOR:
            out_specs.append(pl.BlockSpec((1, N_H, tm, w), lambda i, j: (i, 0, j, 0)))
            out_shape.append(jax.ShapeDtypeStruct((b, N_H, n, w), dt))
        else:
            out_specs.append(pl.BlockSpec((1, tm, w), lambda i, j: (i, j, 0)))
            out_shape.append(jax.ShapeDtypeStruct((b, n, w), dt))
    outs = pl.pallas_call(
        functools.partial(_proj_kernel, rotate=rotate),
        grid=(b, n // tm),
        in_specs=in_specs,
        out_specs=out_specs,
        out_shape=out_shape,
        compiler_params=_params("parallel", "parallel"),
        name="in_proj_rot" if rotate else "in_proj",
    )(*args)
    return {name: o for (name, _, _), o in zip(_PROJ_OUT, outs)}


def _attn_kernel(*refs, diff, has_lat, tk, lambda_init):
    q_ref, kc_ref, vc_ref = refs[:3]
    refs = refs[3:]
    if has_lat:
        kl_ref, vl_ref = refs[:2]
        refs = refs[2:]
    if diff:
        lam_ref, g_ref = refs[:2]
        refs = refs[2:]
    if diff:
        o_ref, acc_ref, qst_ref = refs
    else:
        o_ref, acc_ref = refs
    tq = o_ref.shape[1]
    n_maps = 2 if diff else 1

    if diff:
        q = q_ref[0]
        lane_q = _iota((1, q.shape[-1]), 1)
        for i in range(2 * N_H):
            qst_ref[i * tq:(i + 1) * tq, :] = jnp.where((lane_q // DIFF_QK) == i, q, jnp.zeros_like(q))

    def scores(k_ref, rows):
        if diff:
            return [(_dot_nt(qst_ref[...], k_ref[0, rows, :]), 0)]
        return [(_dot_nt(q_ref[0, h], k_ref[0, h, rows, :]), h * tq) for h in range(N_H)]

    def softmax_pv(blocks, v_ref, rows, ms):
        new = []
        for g, (s, r0) in enumerate(blocks):
            mx = jnp.max(s, axis=-1, keepdims=True)
            if ms is not None:
                mx = jnp.maximum(ms[g], mx)
                alpha = jnp.exp2(ms[g] - mx)
            p = jnp.exp2(s - mx).astype(BF16)
            hrows = n_maps * tq
            for j in range(s.shape[0] // hrows):
                h = (r0 + j * hrows) // hrows
                pv = _dot(p[j * hrows:(j + 1) * hrows], v_ref[0, h, rows, :])
                arows = slice(r0 + j * hrows, r0 + (j + 1) * hrows)
                if ms is None:
                    acc_ref[arows, :] = pv
                else:
                    acc_ref[arows, :] = alpha[j * hrows:(j + 1) * hrows] * acc_ref[arows, :] + pv
            new.append(mx)
        return tuple(new)

    ctx_scores = scores(kc_ref, slice(None))
    if has_lat:
        nxt = scores(kl_ref, slice(0, tk))
    ms = softmax_pv(ctx_scores, vc_ref, slice(None), None)
    if has_lat:
        n_blocks = kl_ref.shape[-2] // tk
        for c in range(n_blocks):
            cur = nxt
            if c + 1 < n_blocks:
                nxt = scores(kl_ref, slice((c + 1) * tk, (c + 2) * tk))
            ms = softmax_pv(cur, vl_ref, slice(c * tk, (c + 1) * tk), ms)

    def head_out(i):
        a = acc_ref[i * tq:(i + 1) * tq, :]
        return a[:, :HEAD_DIM] * (1.0 / a[:, V_ONES:V_ONES + 1])

    if diff:
        lv = lam_ref[...]
        lam = (jnp.exp(jnp.sum(lv[0:1] * lv[1:2], axis=-1, keepdims=True))
               - jnp.exp(jnp.sum(lv[2:3] * lv[3:4], axis=-1, keepdims=True)) + lambda_init)
        out = jnp.concatenate([head_out(2 * h) - lam * head_out(2 * h + 1) for h in range(N_H)], axis=1)
        out = _group_norm(out, g_ref[...], _group_mean_matrix(GROUP_W, HEAD_DIM)) * (1.0 - lambda_init)
    else:
        out = jnp.concatenate([head_out(h) for h in range(N_H)], axis=1)
    o_ref[0] = out.astype(o_ref.dtype)


def _attn_call(q, kc, vc, kl, vl, diff_args, tq, tk, name):
    has_lat = kl is not None
    diff = diff_args is not None
    b, lq = q.shape[0], q.shape[-2]

    def spec(arr, rows, tiled):
        if arr.ndim == 4:
            return pl.BlockSpec((1, N_H, rows, arr.shape[3]), (lambda i, j: (i, 0, j, 0)) if tiled else (lambda i, j: (i, 0, 0, 0)))
        return pl.BlockSpec((1, rows, arr.shape[2]), (lambda i, j: (i, j, 0)) if tiled else (lambda i, j: (i, 0, 0)))

    in_specs = [spec(q, tq, True), spec(kc, kc.shape[-2], False), spec(vc, vc.shape[-2], False)]
    args = [q, kc, vc]
    if has_lat:
        in_specs += [spec(kl, kl.shape[-2], False), spec(vl, vl.shape[-2], False)]
        args += [kl, vl]
    lambda_init = 0.0
    if diff:
        lam_vecs, sub_g, lambda_init = diff_args
        in_specs += [_const_spec(lam_vecs.shape), _const_spec(sub_g.shape)]
        args += [lam_vecs, sub_g]
    return pl.pallas_call(
        functools.partial(_attn_kernel, diff=diff, has_lat=has_lat, tk=tk, lambda_init=lambda_init),
        grid=(b, lq // tq),
        in_specs=in_specs,
        out_specs=pl.BlockSpec((1, tq, GROUP_W), lambda i, j: (i, j, 0)),
        out_shape=jax.ShapeDtypeStruct((b, lq, GROUP_W), BF16),
        scratch_shapes=[pltpu.VMEM((N_H * (2 if diff else 1) * tq, V_SLOT), F32)]
                       + ([pltpu.VMEM((2 * N_H * tq, q.shape[-1]), BF16)] if diff else []),
        compiler_params=_params("parallel", "parallel"),
        name=name,
    )(*args)


def _gla_chunks(work, consts):
    tri_f, tri_b, rexp, bdmask, rowid = consts
    c, sb = GLA_CHUNK, GLA_SUB
    qk_tiles, tiles_per_sb, n_sb = c // 8, sb // 8, c // sb
    lane_k = _iota((1, 128), 1)
    lane_o = _iota((1, GROUP_W), 1)

    bs = [_dot(tri_b if rev else tri_f, a, precision=HIGHEST) * LOG2E for (_, _, _, a, _, rev) in work]

    e_alls, spans_all, qms, kss, q_ins, k_ends, b_ends = [], [], [], [], [], [], []
    for (q, k, v, a, st, rev), b in zip(work, bs):
        pieces, spans = [], []
        for j in range(c):
            t, blk = j // 8, j // sb
            t0, t1 = (blk * tiles_per_sb, t + 1) if rev else (t, (blk + 1) * tiles_per_sb)
            for tt in range(t0, t1):
                rows = slice(8 * tt, 8 * tt + 8)
                arg = b[rows] - b[j:j + 1]
                if tt == t:
                    keep = (rowid[rows] <= j) if rev else (rowid[rows] >= j)
                    arg = jnp.where(keep, arg, NEG_BIG)
                pieces.append(jnp.exp2(arg) * (q[rows] * k[j:j + 1]))
            spans.append((t0, t1))
        e_alls.append(jnp.concatenate(pieces, axis=0).astype(BF16))
        spans_all.append(spans)
        qm_d, ks_d = [], []
        for blk in (range(n_sb - 1) if rev else range(1, n_sb)):
            r0 = blk * sb
            if rev:
                ref_row, seen = b[r0 + sb:r0 + sb + 1], rowid >= r0 + sb
            else:
                ref_row, seen = b[r0 - 1:r0], rowid < r0
            qs = q[r0:r0 + sb] * jnp.exp2(b[r0:r0 + sb] - ref_row)
            ks_d.append((k * jnp.exp2(jnp.where(seen, ref_row - b, NEG_BIG))).astype(BF16))
            qm_d.append(jnp.concatenate([jnp.where((lane_k // GLA_DK) == h, qs, 0.0) for h in range(N_H)],
                                        axis=0).astype(BF16))
        qms.append(qm_d)
        kss.append(ks_d)
        b_end = b[0:1] if rev else b[c - 1:c]
        b_ends.append(b_end)
        q_ins.append((q * jnp.exp2(b)).astype(BF16))
        k_ends.append((k * jnp.exp2(b_end - b)).astype(BF16))

    sxs = [_dot(e_all, rexp) for e_all in e_alls]
    s_ods = [jnp.concatenate([_dot_nt(qm, ks) for qm, ks in zip(qm_d, ks_d)], axis=0).astype(BF16)
             for qm_d, ks_d in zip(qms, kss)]
    o_ins = [_dot_nt(q_in, w[4].astype(BF16)) for q_in, w in zip(q_ins, work)]
    upds = [_dot_tn(w[2].astype(BF16), k_end) for k_end, w in zip(k_ends, work)]

    ress = [_dot(s_od, w[2].astype(BF16)) for s_od, w in zip(s_ods, work)]

    outs = []
    for d, (q, k, v, a, st, rev) in enumerate(work):
        acc = [jnp.zeros((8, GROUP_W), F32) for _ in range(qk_tiles)]
        v32 = v.astype(F32)
        off = 0
        for j in range(c):
            t0, t1 = spans_all[d][j]
            vj = v32[j:j + 1]
            for t in range(t0, t1):
                acc[t] = acc[t] + sxs[d][off:off + 8] * vj
                off += 8
        for n, blk in enumerate(range(n_sb - 1) if rev else range(1, n_sb)):
            o_blk = jnp.zeros((sb, GROUP_W), F32)
            for h in range(N_H):
                r0 = (n * N_H + h) * sb
                o_blk = o_blk + jnp.where((lane_o // HEAD_DIM) == h, ress[d][r0:r0 + sb], 0.0)
            for tt in range(tiles_per_sb):
                t = blk * tiles_per_sb + tt
                acc[t] = acc[t] + o_blk[8 * tt:8 * tt + 8]
        o = jnp.concatenate(acc, axis=0) + o_ins[d]
        outs.append((o, st * jnp.exp2(b_ends[d]) + jnp.where(bdmask, upds[d], 0.0)))
    return outs


def _gla_kernel(qc_ref, kc_ref, vc_ref, afc_ref, abc_ref, ql_ref, kl_ref, vl_ref, afl_ref, abl_ref,
                oc_ref, ol_ref, sf_ref, sb_ref):
    c = GLA_CHUNK
    ri, ci = _iota((c, c), 0), _iota((c, c), 1)
    tri_f = jnp.where(ri >= ci, 1.0, 0.0).astype(F32)
    tri_b = jnp.where(ri <= ci, 1.0, 0.0).astype(F32)
    rexp = jnp.where((_iota((128, GROUP_W), 0) // GLA_DK) == (_iota((128, GROUP_W), 1) // HEAD_DIM),
                     1.0, 0.0).astype(BF16)
    bdmask = (_iota((GROUP_W, 128), 0) // HEAD_DIM) == (_iota((GROUP_W, 128), 1) // GLA_DK)
    rowid = _iota((c, 1), 0)
    consts = (tri_f, tri_b, rexp, bdmask, rowid)

    sf_ref[...] = jnp.zeros_like(sf_ref)
    sb_ref[...] = jnp.zeros_like(sb_ref)
    oc_ref[...] = jnp.zeros_like(oc_ref)
    ol_ref[...] = jnp.zeros_like(ol_ref)

    def sweep(q_ref, k_ref, v_ref, af_ref, ab_ref, o_ref):
        n = q_ref.shape[1] // c

        def body(i, carry):
            rows_f = pl.ds(pl.multiple_of(i * c, c), c)
            rows_b = pl.ds(pl.multiple_of((n - 1 - i) * c, c), c)
            work = [(q_ref[0, rows, :], k_ref[0, rows, :], v_ref[0, rows, :], a_ref[0, rows, :], st_ref[...], rev)
                    for rows, a_ref, st_ref, rev in ((rows_f, af_ref, sf_ref, False), (rows_b, ab_ref, sb_ref, True))]
            (o_f, st_f), (o_b, st_b) = _gla_chunks(work, consts)
            sf_ref[...] = st_f
            sb_ref[...] = st_b
            o_ref[0, rows_f, :] += o_f
            o_ref[0, rows_b, :] += o_b
            return carry

        lax.fori_loop(0, n, body, 0)

    sweep(qc_ref, kc_ref, vc_ref, afc_ref, abc_ref, oc_ref)
    sweep(ql_ref, kl_ref, vl_ref, afl_ref, abl_ref, ol_ref)


def _seq_spec(arr):
    return pl.BlockSpec((1,) + arr.shape[1:], lambda i: (i, 0, 0))


def _gla_call(pc, plat):
    b = pc["gq"].shape[0]
    args = [pc["gq"], pc["gk"], pc["gv"], pc["gaf"], pc["gab"],
            plat["gq"], plat["gk"], plat["gv"], plat["gaf"], plat["gab"]]
    lc, ll = pc["gq"].shape[1], plat["gq"].shape[1]
    return pl.pallas_call(
        _gla_kernel,
        grid=(b,),
        in_specs=[_seq_spec(a) for a in args],
        out_specs=[pl.BlockSpec((1, lc, GROUP_W), lambda i: (i, 0, 0)),
                   pl.BlockSpec((1, ll, GROUP_W), lambda i: (i, 0, 0))],
        out_shape=[jax.ShapeDtypeStruct((b, lc, GROUP_W), F32), jax.ShapeDtypeStruct((b, ll, GROUP_W), F32)],
        scratch_shapes=[pltpu.VMEM((GROUP_W, 128), F32), pltpu.VMEM((GROUP_W, 128), F32)],
        compiler_params=_params("parallel"),
        name="gla_scan",
    )(*args)


def _ret_kernel(lg_ref, qc_ref, kc_ref, vc_ref, ql_ref, kl_ref, vl_ref, oc_ref, ol_ref,
                sf_ref, sb_ref, dm_ref):
    c = RET_CHUNK
    lgam = _log_sigmoid(lg_ref[...])
    ri, ci = _iota((c, c), 0), _iota((c, c), 1)
    pos = _iota((c, 128), 0).astype(F32)
    dq, dk, dc = [], [], []
    for d in range(2):
        lg = lgam[d:d + 1]
        dist = (ri - ci) if d == 0 else (ci - ri)
        for h in range(N_H):
            lg_h = lg[:, h * RET_DK:h * RET_DK + 1]
            dm_ref[d, h] = jnp.exp(jnp.where(dist >= 0, dist.astype(F32) * lg_h, NEG_BIG))
        dq.append(jnp.exp(((pos + 1.0) if d == 0 else (c - pos)) * lg))
        dk.append(jnp.exp(((c - 1.0 - pos) if d == 0 else pos) * lg))
        dc.append(jnp.exp(c * lg))
    bdmask = (_iota((GROUP_W, 128), 0) // HEAD_DIM) == (_iota((GROUP_W, 128), 1) // RET_DK)
    lane_q = _iota((1, 128), 1)
    lane_o = _iota((1, GROUP_W), 1)

    sf_ref[...] = jnp.zeros_like(sf_ref)
    sb_ref[...] = jnp.zeros_like(sb_ref)
    oc_ref[...] = jnp.zeros_like(oc_ref)
    ol_ref[...] = jnp.zeros_like(ol_ref)

    def chunk(q, k, v, st_ref, d):
        st = st_ref[...]
        k16, v16 = k.astype(BF16), v.astype(BF16)
        o = _dot_nt((q * dq[d]).astype(BF16), st.astype(BF16))
        for h in range(N_H):
            qh = jnp.where((lane_q // RET_DK) == h, q, 0.0).astype(BF16)
            s = _dot_nt(qh, k16) * dm_ref[d, h]
            o = o + jnp.where((lane_o // HEAD_DIM) == h, _dot(s.astype(BF16), v16), 0.0)
        upd = _dot_tn(v16, (k * dk[d]).astype(BF16))
        st_ref[...] = st * dc[d] + jnp.where(bdmask, upd, 0.0)
        return o

    def sweep(q_ref, k_ref, v_ref, o_ref):
        n = q_ref.shape[1] // c

        def body(i, carry):
            for d, idx, st_ref in ((0, i, sf_ref), (1, n - 1 - i, sb_ref)):
                rows = pl.ds(pl.multiple_of(idx * c, c), c)
                o_ref[0, rows, :] += chunk(q_ref[0, rows, :], k_ref[0, rows, :], v_ref[0, rows, :], st_ref, d)
            return carry

        lax.fori_loop(0, n, body, 0)

    sweep(qc_ref, kc_ref, vc_ref, oc_ref)
    sweep(ql_ref, kl_ref, vl_ref, ol_ref)


def _ret_call(lg, pc, plat):
    b = pc["rq"].shape[0]
    args = [pc["rq"], pc["rk"], pc["rv"], plat["rq"], plat["rk"], plat["rv"]]
    lc, ll = pc["rq"].shape[1], plat["rq"].shape[1]
    return pl.pallas_call(
        _ret_kernel,
        grid=(b,),
        in_specs=[_const_spec(lg.shape)] + [_seq_spec(a) for a in args],
        out_specs=[pl.BlockSpec((1, lc, GROUP_W), lambda i: (i, 0, 0)),
                   pl.BlockSpec((1, ll, GROUP_W), lambda i: (i, 0, 0))],
        out_shape=[jax.ShapeDtypeStruct((b, lc, GROUP_W), F32), jax.ShapeDtypeStruct((b, ll, GROUP_W), F32)],
        scratch_shapes=[pltpu.VMEM((GROUP_W, 128), F32), pltpu.VMEM((GROUP_W, 128), F32),
                        pltpu.VMEM((2, N_H, RET_CHUNK, RET_CHUNK), F32)],
        compiler_params=_params("parallel"),
        name="ret_scan",
    )(lg, *args)


def _mix_out_kernel(x_ref, mod_ref, ng_ref, do_ref, mo_ref, go_ref, gr_ref, ro_ref, rg_ref,
                    gg_ref, rgn_ref, wo_ref, x1_ref, h2_ref):
    gmat = _group_mean_matrix(GROUP_W, HEAD_DIM)
    gl = _group_norm(go_ref[0], gg_ref[...], gmat) * _silu(gr_ref[0].astype(F32))
    rt = _group_norm(ro_ref[0], rgn_ref[...], gmat) * _silu(rg_ref[0].astype(F32))
    ml = (_dot(do_ref[0], wo_ref[0:256]) + _dot(mo_ref[0], wo_ref[256:512])
          + _dot(gl.astype(BF16), wo_ref[512:768]) + _dot(rt.astype(BF16), wo_ref[768:1024]))
    x1 = x_ref[0] + mod_ref[0, 2:3] * _rms(ml, ng_ref[1:2])
    x1_ref[0] = x1
    h2_ref[0] = (_rms(x1, ng_ref[2:3]) * (1.0 + mod_ref[0, 4:5]) + mod_ref[0, 3:4]).astype(BF16)


def _mod_spec(mod, d):
    if mod.shape[0] > 1:
        return pl.BlockSpec((1, 6, d), lambda i, j: (i, 0, 0))
    return pl.BlockSpec((1, 6, d), lambda i, j: (0, 0, 0))


def _mix_out_call(x, mod, ng, d_o, m_o, g_o, g_r, r_o, r_g, lw, tm):
    b, n, d = x.shape

    def tok(w):
        return pl.BlockSpec((1, tm, w), lambda i, j: (i, j, 0))

    return pl.pallas_call(
        _mix_out_kernel,
        grid=(b, n // tm),
        in_specs=[tok(d), _mod_spec(mod, d), _const_spec(ng.shape)] + [tok(GROUP_W)] * 6
                 + [_const_spec(lw["gla_g"].shape), _const_spec(lw["ret_g"].shape), _const_spec(lw["w_out"].shape)],
        out_specs=[tok(d), tok(d)],
        out_shape=[jax.ShapeDtypeStruct((b, n, d), F32), jax.ShapeDtypeStruct((b, n, d), BF16)],
        compiler_params=_params("parallel", "parallel"),
        name="mix_out",
    )(x, mod, ng, d_o, m_o, g_o, g_r, r_o, r_g, lw["gla_g"], lw["ret_g"], lw["w_out"])


def _ffn_kernel(h_ref, x1_ref, mod_ref, ng_ref, wg_ref, wu_ref, wo_ref, x2_ref):
    h = h_ref[0]
    a = (_silu(_dot(h, wg_ref[...])) * _dot(h, wu_ref[...])).astype(BF16)
    y = _dot(a, wo_ref[...])
    x2_ref[0] = x1_ref[0] + mod_ref[0, 5:6] * _rms(y, ng_ref[3:4])


def _ffn_call(h2, x1, mod, ng, lw, tm):
    b, n, d = x1.shape

    def tok(w):
        return pl.BlockSpec((1, tm, w), lambda i, j: (i, j, 0))

    return pl.pallas_call(
        _ffn_kernel,
        grid=(b, n // tm),
        in_specs=[tok(d), tok(d), _mod_spec(mod, d), _const_spec(ng.shape),
                  _const_spec(lw["ffn_g"].shape), _const_spec(lw["ffn_u"].shape), _const_spec(lw["ffn_o"].shape)],
        out_specs=tok(d),
        out_shape=jax.ShapeDtypeStruct((b, n, d), F32),
        compiler_params=_params("parallel", "parallel"),
        name="ffn",
    )(h2, x1, mod, ng, lw["ffn_g"], lw["ffn_u"], lw["ffn_o"])


def _pack_layer(l, w_in, diff_norm, mla_q_norm, mla_kv_norm, mla_w_uq, mla_w_ukv, gla_w_gate, gla_b_gate,
                gla_norm, ret_decay, ret_norm, w_out, ffn_w_in, ffn_w_out):
    d = w_in.shape[1]
    w = w_in[l]
    sizes = (256, 256, 256, 256, 128, 32, 128, 128, 256, 256, 16, 16, 128, 128, 256, 256)
    offs = np.concatenate([[0], np.cumsum(sizes)])
    seg = [w[:, offs[i]:offs[i + 1]] for i in range(len(sizes))]
    z = lambda n: jnp.zeros((d, n), w.dtype)
    misc = jnp.concatenate([seg[10], seg[11], z(32), seg[5], z(32)], axis=1)
    wm = jnp.concatenate(seg[0:5] + [misc] + seg[6:10] + seg[12:16], axis=1).astype(BF16)

    uq = mla_w_uq[l].reshape(-1, N_H, MLA_NOPE + MLA_ROPE)
    uq = jnp.pad(uq, ((0, 0), (0, 0), (0, MLA_SLOT - MLA_NOPE - MLA_ROPE))).reshape(-1, N_H * MLA_SLOT)
    ukv = mla_w_ukv[l].reshape(-1, N_H, MLA_NOPE + HEAD_DIM)
    uk = jnp.pad(ukv[:, :, :MLA_NOPE], ((0, 0), (0, 0), (0, MLA_SLOT - MLA_NOPE))).reshape(-1, N_H * MLA_SLOT)
    uv = ukv[:, :, MLA_NOPE:].reshape(-1, GROUP_W)

    wgf = jnp.zeros((128, 128), F32).at[0:GLA_GATE_RANK].set(gla_w_gate[l, 0]).astype(BF16)
    wgb = jnp.zeros((128, 128), F32).at[GLA_GATE_RANK:2 * GLA_GATE_RANK].set(gla_w_gate[l, 1]).astype(BF16)
    hid = ffn_w_out.shape[1]
    return {
        "wm": wm, "qg": mla_q_norm[l][None], "kvg": mla_kv_norm[l][None],
        "wuq": uq.astype(BF16), "wuk": uk.astype(BF16), "wuv": uv.astype(BF16),
        "wgf": wgf, "wgb": wgb, "bg": gla_b_gate[l],
        "diff_g": jnp.tile(diff_norm[l], N_H)[None], "gla_g": jnp.tile(gla_norm[l], N_H)[None],
        "ret_g": jnp.tile(ret_norm[l], N_H)[None],
        "ret_lg": jnp.repeat(ret_decay[l], RET_DK, axis=1),
        "w_out": w_out[l].astype(BF16),
        "ffn_g": ffn_w_in[l][:, :hid].astype(BF16), "ffn_u": ffn_w_in[l][:, hid:].astype(BF16),
        "ffn_o": ffn_w_out[l].astype(BF16),
    }


def _rope_tables(seq):
    rows = seq // GRID_W
    row = jnp.repeat(jnp.arange(rows, dtype=F32), GRID_W)
    col = jnp.tile(jnp.arange(GRID_W, dtype=F32), rows)
    n_freq = ROT_DIM // 4
    freqs = ROPE_THETA ** (-jnp.arange(n_freq, dtype=F32) / n_freq)
    ang = jnp.concatenate([row[:, None] * freqs, col[:, None] * freqs], axis=-1)
    cos, sin = jnp.cos(ang), jnp.sin(ang)
    c32 = jnp.concatenate([cos, cos], axis=-1)
    s32 = jnp.concatenate([-sin, sin], axis=-1)
    one, zero = jnp.ones((seq, 1), F32), jnp.zeros((seq, 1), F32)
    slot_c = jnp.concatenate([jnp.tile(one, (1, MLA_NOPE)), c32, jnp.tile(one, (1, MLA_SLOT - MLA_NOPE - MLA_ROPE))], -1)
    slot_s = jnp.concatenate([jnp.tile(zero, (1, MLA_NOPE)), s32, jnp.tile(zero, (1, MLA_SLOT - MLA_NOPE - MLA_ROPE))], -1)
    return (jnp.tile(c32, (1, 256 // ROT_DIM)), jnp.tile(s32, (1, 256 // ROT_DIM)),
            jnp.tile(slot_c, (1, N_H)), jnp.tile(slot_s, (1, N_H)))


def _tile(n, want):
    t = min(n, want)
    while n % t:
        t //= 2
    return t


def kernel(x, c, ctx, c_ctx, ada_w, ada_b, norm_g, w_in, diff_lam, diff_norm, mla_q_norm, mla_kv_norm,
           mla_w_uq, mla_w_ukv, gla_w_gate, gla_b_gate, gla_norm, ret_decay, ret_norm, w_out, ffn_w_in, ffn_w_out):
    b, seq, d = x.shape
    lc = ctx.shape[1]
    depth = ada_w.shape[0]
    rope_tabs = _rope_tables(seq)

    pad = (-(b + 1)) % 8
    s_all = jnp.concatenate([c, c_ctx[None], jnp.zeros((pad, d), F32)], axis=0)
    mods = _ada_call(s_all, ada_w, ada_b)

    tm_l, tm_c = _tile(seq, 512), _tile(lc, 256)
    tq_c = _tile(lc, 256)
    tq_d, tk_d = _tile(seq, 256), _tile(seq, 1024)
    tq_m, tk_m = _tile(seq, 512), _tile(seq, 1024)

    xl, xc = x, ctx
    for l in range(depth):
        with_ctx = l < depth - 1
        lambda_init = 0.8 - 0.6 * math.exp(-0.3 * l)
        lw = _pack_layer(l, w_in, diff_norm, mla_q_norm, mla_kv_norm, mla_w_uq, mla_w_ukv, gla_w_gate,
                         gla_b_gate, gla_norm, ret_decay, ret_norm, w_out, ffn_w_in, ffn_w_out)
        mod_l = mods[l, :b].reshape(b, 6, d)
        mod_c = mods[l, b:b + 1].reshape(1, 6, d)
        ng = norm_g[l]

        pc = _proj_call(xc, mod_c, ng, lw, None, tm_c)
        pt = _proj_call(xl, mod_l, ng, lw, rope_tabs, tm_l)

        dargs = (diff_lam[l], lw["diff_g"], lambda_init)
        d_l = _attn_call(pt["dq"], pc["dk"], pc["dv"], pt["dk"], pt["dv"], dargs, tq_d, tk_d, "diff_attn")
        m_l = _attn_call(pt["mq"], pc["mk"], pc["mv"], pt["mk"], pt["mv"], None, tq_m, tk_m, "mla_attn")
        g_c, g_l = _gla_call(pc, pt)
        r_c, r_l = _ret_call(lw["ret_lg"], pc, pt)

        x1, h2 = _mix_out_call(xl, mod_l, ng, d_l, m_l, g_l, pt["gr"], r_l, pt["rg"], lw, tm_l)
        xl = _ffn_call(h2, x1, mod_l, ng, lw, tm_l)
        if with_ctx:
            d_c = _attn_call(pc["dq"], pc["dk"], pc["dv"], None, None, dargs, tq_c, tk_d, "diff_attn_ctx")
            m_c = _attn_call(pc["mq"], pc["mk"], pc["mv"], None, None, None, tq_c, tk_m, "mla_attn_ctx")
            x1c, h2c = _mix_out_call(xc, mod_c, ng, d_c, m_c, g_c, pc["gr"], r_c, pc["rg"], lw, tm_c)
            xc = _ffn_call(h2c, x1c, mod_c, ng, lw, tm_c)
    return xl
```

```python
import functools
import math

import jax
import jax.numpy as jnp
import numpy as np
from jax import lax
from jax.experimental import pallas as pl
from jax.experimental.pallas import tpu as pltpu

GRID_W = 64
N_H = 4
HEAD_DIM = 64
GROUP_W = N_H * HEAD_DIM
ROT_DIM = 32
ROPE_THETA = 10000.0
DIFF_QK = 32
MLA_NOPE = 64
MLA_ROPE = 32
MLA_SLOT = 128
V_SLOT = 128
V_ONES = HEAD_DIM
LOG2E = math.log2(math.e)
GLA_DK = 32
GLA_GATE_RANK = 16
GLA_GATE_NORM = 16.0
GLA_CHUNK = 64
GLA_SUB = 16
RET_DK = 32
RET_CHUNK = 256
RMS_EPS = 1e-6
NEG_BIG = -1e30

VMEM_LIMIT_BYTES = 48 * 1024 * 1024
F32 = jnp.float32
BF16 = jnp.bfloat16
HIGHEST = lax.Precision.HIGHEST

C_DQ, C_DK, C_DV, C_CQ, C_CKV, C_MISC = 0, 256, 512, 768, 1024, 1152
C_GQ, C_GK, C_GV, C_GR, C_RQ, C_RK, C_RV, C_RG, C_END = 1280, 1408, 1536, 1792, 2048, 2176, 2304, 2560, 2816
MISC_KR = 64


def _dot(a, b, precision=None):
    return jnp.dot(a, b, preferred_element_type=F32, precision=precision)


def _dot_nt(a, b):
    return lax.dot_general(a, b, (((1,), (1,)), ((), ())), preferred_element_type=F32)


def _dot_tn(a, b):
    return lax.dot_general(a, b, (((0,), (0,)), ((), ())), preferred_element_type=F32)


def _rms(x, g):
    ms = jnp.mean(x * x, axis=-1, keepdims=True)
    return x * lax.rsqrt(ms + RMS_EPS) * g


def _silu(x):
    return x * (1.0 / (1.0 + jnp.exp(-x)))


def _log_sigmoid(x):
    return jnp.minimum(x, 0.0) - jnp.log1p(jnp.exp(-jnp.abs(x)))


def _iota(shape, dim):
    return lax.broadcasted_iota(jnp.int32, shape, dim)


def _group_mean_matrix(width, group):
    same = (_iota((width, width), 0) // group) == (_iota((width, width), 1) // group)
    return jnp.where(same, 1.0 / group, 0.0).astype(BF16)


def _group_norm(o, g, gmat):
    ms = _dot((o * o).astype(BF16), gmat)
    return o * lax.rsqrt(ms + RMS_EPS) * g


def _rope(t, cs, sn):
    w = t.shape[-1]
    lo = (_iota((1, w), 1) % ROT_DIM) < (ROT_DIM // 2)
    swapped = jnp.where(lo, pltpu.roll(t, w - ROT_DIM // 2, 1), pltpu.roll(t, ROT_DIM // 2, 1))
    return t * cs + swapped * sn


def _const_spec(shape):
    nd = len(shape)
    return pl.BlockSpec(shape, lambda *_: (0,) * nd)


def _params(*sem):
    return pltpu.CompilerParams(dimension_semantics=sem, vmem_limit_bytes=VMEM_LIMIT_BYTES)


def _ada_kernel(s_ref, w_ref, b_ref, o_ref):
    s = _silu(s_ref[...])
    o_ref[0] = _dot(s, w_ref[0], precision=HIGHEST) + b_ref[0]


def _ada_call(s_all, ada_w, ada_b):
    depth, d, n = ada_w.shape
    rows = s_all.shape[0]
    tn = 1024
    return pl.pallas_call(
        _ada_kernel,
        grid=(depth, n // tn),
        in_specs=[pl.BlockSpec((rows, d), lambda l, j: (0, 0)),
                  pl.BlockSpec((1, d, tn), lambda l, j: (l, 0, j)),
                  pl.BlockSpec((1, 1, tn), lambda l, j: (l, 0, j))],
        out_specs=pl.BlockSpec((1, rows, tn), lambda l, j: (l, 0, j)),
        out_shape=jax.ShapeDtypeStruct((depth, rows, n), F32),
        compiler_params=_params("parallel", "parallel"),
        name="ada_mod",
    )(s_all, ada_w, ada_b.reshape(depth, 1, n))


def _proj_kernel(*refs, rotate):
    (x_ref, mod_ref, ng_ref, wm_ref, qg_ref, kvg_ref, wuq_ref, wuk_ref, wuv_ref,
     wgf_ref, wgb_ref, bg_ref) = refs[:12]
    refs = refs[12:]
    if rotate:
        cs256_ref, sn256_ref, cs512_ref, sn512_ref = refs[:4]
        refs = refs[4:]
    (dq_ref, dk_ref, dv_ref, mq_ref, mk_ref, mv_ref, gq_ref, gk_ref, gv_ref, gr_ref,
     gaf_ref, gab_ref, rq_ref, rk_ref, rv_ref, rg_ref) = refs

    x = x_ref[0]
    h = (_rms(x, ng_ref[0:1]) * (1.0 + mod_ref[0, 1:2]) + mod_ref[0, 0:1]).astype(BF16)

    def proj(lo, hi):
        return _dot(h, wm_ref[:, lo:hi])

    def rope256(t):
        return _rope(t, cs256_ref[...], sn256_ref[...]) if rotate else t

    def rope128(t):
        return _rope(t, cs256_ref[:, :128], sn256_ref[:, :128]) if rotate else t

    def rope512(t):
        return _rope(t, cs512_ref[...], sn512_ref[...]) if rotate else t

    lane = _iota((1, 128), 1)
    ones_lane = jnp.where(lane == V_ONES, 1.0, 0.0)

    def store_values(v_ref, v):
        rolled = pltpu.roll(v, GROUP_W - HEAD_DIM, 1)
        for hd in range(N_H):
            src = v if hd % 2 == 0 else rolled
            blk = src[:, (hd // 2) * 128:(hd // 2 + 1) * 128]
            v_ref[0, hd] = jnp.where(lane < HEAD_DIM, blk, ones_lane).astype(BF16)

    dq_ref[0] = (rope256(proj(C_DQ, C_DK)) * (DIFF_QK ** -0.5 * LOG2E)).astype(BF16)
    dk_ref[0] = rope256(proj(C_DK, C_DV)).astype(BF16)
    store_values(dv_ref, proj(C_DV, C_CQ))

    qn = _rms(proj(C_CQ, C_CKV), qg_ref[...]).astype(BF16)
    q512 = (rope512(_dot(qn, wuq_ref[...])) * ((MLA_NOPE + MLA_ROPE) ** -0.5 * LOG2E)).astype(BF16)
    kvn = _rms(proj(C_CKV, C_MISC), kvg_ref[...]).astype(BF16)
    misc = proj(C_MISC, C_GQ)
    kr = jnp.where((lane >= MISC_KR) & (lane < MISC_KR + MLA_ROPE), misc, 0.0)
    k512 = rope512(_dot(kvn, wuk_ref[...]) + jnp.concatenate([kr] * N_H, axis=1)).astype(BF16)
    for hd in range(N_H):
        mq_ref[0, hd] = q512[:, hd * MLA_SLOT:(hd + 1) * MLA_SLOT]
        mk_ref[0, hd] = k512[:, hd * MLA_SLOT:(hd + 1) * MLA_SLOT]
    store_values(mv_ref, _dot(kvn, wuv_ref[...]))

    gq_ref[0] = proj(C_GQ, C_GK) * GLA_DK ** -0.5
    gk_ref[0] = proj(C_GK, C_GV)
    gv_ref[0] = proj(C_GV, C_GR).astype(BF16)
    gr_ref[0] = proj(C_GR, C_RQ).astype(BF16)
    misc16 = misc.astype(BF16)
    gaf_ref[0] = _log_sigmoid(_dot(misc16, wgf_ref[...]) + bg_ref[0:1]) / GLA_GATE_NORM
    gab_ref[0] = _log_sigmoid(_dot(misc16, wgb_ref[...]) + bg_ref[1:2]) / GLA_GATE_NORM

    rq_ref[0] = rope128(proj(C_RQ, C_RK))
    rk_ref[0] = rope128(proj(C_RK, C_RV) * RET_DK ** -0.5)
    rv_ref[0] = proj(C_RV, C_RG).astype(BF16)
    rg_ref[0] = proj(C_RG, C_END).astype(BF16)


_HEAD_MAJOR = ("mq", "mk", "dv", "mv")
_PROJ_OUT = (("dq", 256, BF16), ("dk", 256, BF16), ("dv", V_SLOT, BF16),
             ("mq", MLA_SLOT, BF16), ("mk", MLA_SLOT, BF16), ("mv", V_SLOT, BF16),
             ("gq", 128, F32), ("gk", 128, F32), ("gv", 256, BF16), ("gr", 256, BF16),
             ("gaf", 128, F32), ("gab", 128, F32),
             ("rq", 128, F32), ("rk", 128, F32), ("rv", 256, BF16), ("rg", 256, BF16))


def _proj_call(x, mod, ng, lw, rope_tabs, tm):
    b, n, d = x.shape
    rotate = rope_tabs is not None
    per_batch_mod = mod.shape[0] > 1
    weights = (lw["wm"], lw["qg"], lw["kvg"], lw["wuq"], lw["wuk"], lw["wuv"], lw["wgf"], lw["wgb"], lw["bg"])
    in_specs = [pl.BlockSpec((1, tm, d), lambda i, j: (i, j, 0)),
                pl.BlockSpec((1, 6, d), (lambda i, j: (i, 0, 0)) if per_batch_mod else (lambda i, j: (0, 0, 0))),
                _const_spec(ng.shape)]
    in_specs += [_const_spec(w.shape) for w in weights]
    args = [x, mod, ng, *weights]
    if rotate:
        in_specs += [pl.BlockSpec((tm, t.shape[1]), lambda i, j: (j, 0)) for t in rope_tabs]
        args += list(rope_tabs)
    out_specs, out_shape = [], []
    for name, w, dt in _PROJ_OUT:
        if name in _HEAD_MAJOR:
            out_specs.append(pl.BlockSpec((1, N_H, tm, w), lambda i, j: (i, 0, j, 0)))
            out_shape.append(jax.ShapeDtypeStruct((b, N_H, n, w), dt))
        else:
            out_specs.append(pl.BlockSpec((1, tm, w), lambda i, j: (i, j, 0)))
            out_shape.append(jax.ShapeDtypeStruct((b, n, w), dt))
    outs = pl.pallas_call(
        functools.partial(_proj_kernel, rotate=rotate),
        grid=(b, n // tm),
        in_specs=in_specs,
        out_specs=out_specs,
        out_shape=out_shape,
        compiler_params=_params("parallel", "parallel"),
        name="in_proj_rot" if rotate else "in_proj",
    )(*args)
    return {name: o for (name, _, _), o in zip(_PROJ_OUT, outs)}


def _attn_kernel(*refs, diff, has_lat, tk, lambda_init):
    q_ref, kc_ref, vc_ref = refs[:3]
    refs = refs[3:]
    if has_lat:
        kl_ref, vl_ref = refs[:2]
        refs = refs[2:]
    if diff:
        lam_ref, g_ref = refs[:2]
        refs = refs[2:]
    if diff:
        o_ref, acc_ref, qst_ref = refs
    else:
        o_ref, acc_ref = refs
    tq = o_ref.shape[1]
    n_maps = 2 if diff else 1

    if diff:
        q = q_ref[0]
        lane_q = _iota((1, q.shape[-1]), 1)
        for i in range(2 * N_H):
            qst_ref[i * tq:(i + 1) * tq, :] = jnp.where((lane_q // DIFF_QK) == i, q, jnp.zeros_like(q))

    def scores(k_ref, rows):
        if diff:
            return [(_dot_nt(qst_ref[...], k_ref[0, rows, :]), 0)]
        return [(_dot_nt(q_ref[0, h], k_ref[0, h, rows, :]), h * tq) for h in range(N_H)]

    def softmax_pv(blocks, v_ref, rows, ms):
        new = []
        for g, (s, r0) in enumerate(blocks):
            mx = jnp.max(s, axis=-1, keepdims=True)
            if ms is not None:
                mx = jnp.maximum(ms[g], mx)
                alpha = jnp.exp2(ms[g] - mx)
            p = jnp.exp2(s - mx).astype(BF16)
            hrows = n_maps * tq
            for j in range(s.shape[0] // hrows):
                h = (r0 + j * hrows) // hrows
                pv = _dot(p[j * hrows:(j + 1) * hrows], v_ref[0, h, rows, :])
                arows = slice(r0 + j * hrows, r0 + (j + 1) * hrows)
                if ms is None:
                    acc_ref[arows, :] = pv
                else:
                    acc_ref[arows, :] = alpha[j * hrows:(j + 1) * hrows] * acc_ref[arows, :] + pv
            new.append(mx)
        return tuple(new)

    ctx_scores = scores(kc_ref, slice(None))
    if has_lat:
        nxt = scores(kl_ref, slice(0, tk))
    ms = softmax_pv(ctx_scores, vc_ref, slice(None), None)
    if has_lat:
        n_blocks = kl_ref.shape[-2] // tk
        for c in range(n_blocks):
            cur = nxt
            if c + 1 < n_blocks:
                nxt = scores(kl_ref, slice((c + 1) * tk, (c + 2) * tk))
            ms = softmax_pv(cur, vl_ref, slice(c * tk, (c + 1) * tk), ms)

    def head_out(i):
        a = acc_ref[i * tq:(i + 1) * tq, :]
        return a[:, :HEAD_DIM] * (1.0 / a[:, V_ONES:V_ONES + 1])

    if diff:
        lv = lam_ref[...]
        lam = (jnp.exp(jnp.sum(lv[0:1] * lv[1:2], axis=-1, keepdims=True))
               - jnp.exp(jnp.sum(lv[2:3] * lv[3:4], axis=-1, keepdims=True)) + lambda_init)
        out = jnp.concatenate([head_out(2 * h) - lam * head_out(2 * h + 1) for h in range(N_H)], axis=1)
        out = _group_norm(out, g_ref[...], _group_mean_matrix(GROUP_W, HEAD_DIM)) * (1.0 - lambda_init)
    else:
        out = jnp.concatenate([head_out(h) for h in range(N_H)], axis=1)
    o_ref[0] = out.astype(o_ref.dtype)


def _attn_call(q, kc, vc, kl, vl, diff_args, tq, tk, name):
    has_lat = kl is not None
    diff = diff_args is not None
    b, lq = q.shape[0], q.shape[-2]

    def spec(arr, rows, tiled):
        if arr.ndim == 4:
            return pl.BlockSpec((1, N_H, rows, arr.shape[3]), (lambda i, j: (i, 0, j, 0)) if tiled else (lambda i, j: (i, 0, 0, 0)))
        return pl.BlockSpec((1, rows, arr.shape[2]), (lambda i, j: (i, j, 0)) if tiled else (lambda i, j: (i, 0, 0)))

    in_specs = [spec(q, tq, True), spec(kc, kc.shape[-2], False), spec(vc, vc.shape[-2], False)]
    args = [q, kc, vc]
    if has_lat:
        in_specs += [spec(kl, kl.shape[-2], False), spec(vl, vl.shape[-2], False)]
        args += [kl, vl]
    lambda_init = 0.0
    if diff:
        lam_vecs, sub_g, lambda_init = diff_args
        in_specs += [_const_spec(lam_vecs.shape), _const_spec(sub_g.shape)]
        args += [lam_vecs, sub_g]
    return pl.pallas_call(
        functools.partial(_attn_kernel, diff=diff, has_lat=has_lat, tk=tk, lambda_init=lambda_init),
        grid=(b, lq // tq),
        in_specs=in_specs,
        out_specs=pl.BlockSpec((1, tq, GROUP_W), lambda i, j: (i, j, 0)),
        out_shape=jax.ShapeDtypeStruct((b, lq, GROUP_W), BF16),
        scratch_shapes=[pltpu.VMEM((N_H * (2 if diff else 1) * tq, V_SLOT), F32)]
                       + ([pltpu.VMEM((2 * N_H * tq, q.shape[-1]), BF16)] if diff else []),
        compiler_params=_params("parallel", "parallel"),
        name=name,
    )(*args)


def _gla_chunks(work, consts):
    tri_f, tri_b, rexp, bdmask, rowid = consts
    c, sb = GLA_CHUNK, GLA_SUB
    qk_tiles, tiles_per_sb, n_sb = c // 8, sb // 8, c // sb
    lane_k = _iota((1, 128), 1)
    lane_o = _iota((1, GROUP_W), 1)

    bs = [_dot(tri_b if rev else tri_f, a, precision=HIGHEST) * LOG2E for (_, _, _, a, _, rev) in work]

    e_alls, spans_all, qms, kss, q_ins, k_ends, b_ends = [], [], [], [], [], [], []
    for (q, k, v, a, st, rev), b in zip(work, bs):
        pieces, spans = [], []
        for j in range(c):
            t, blk = j // 8, j // sb
            t0, t1 = (blk * tiles_per_sb, t + 1) if rev else (t, (blk + 1) * tiles_per_sb)
            for tt in range(t0, t1):
                rows = slice(8 * tt, 8 * tt + 8)
                arg = b[rows] - b[j:j + 1]
                if tt == t:
                    keep = (rowid[rows] <= j) if rev else (rowid[rows] >= j)
                    arg = jnp.where(keep, arg, NEG_BIG)
                pieces.append(jnp.exp2(arg) * (q[rows] * k[j:j + 1]))
            spans.append((t0, t1))
        e_alls.append(jnp.concatenate(pieces, axis=0).astype(BF16))
        spans_all.append(spans)
        qm_d, ks_d = [], []
        for blk in (range(n_sb - 1) if rev else range(1, n_sb)):
            r0 = blk * sb
            if rev:
                ref_row, seen = b[r0 + sb:r0 + sb + 1], rowid >= r0 + sb
            else:
                ref_row, seen = b[r0 - 1:r0], rowid < r0
            qs = q[r0:r0 + sb] * jnp.exp2(b[r0:r0 + sb] - ref_row)
            ks_d.append((k * jnp.exp2(jnp.where(seen, ref_row - b, NEG_BIG))).astype(BF16))
            qm_d.append(jnp.concatenate([jnp.where((lane_k // GLA_DK) == h, qs, 0.0) for h in range(N_H)],
                                        axis=0).astype(BF16))
        qms.append(qm_d)
        kss.append(ks_d)
        b_end = b[0:1] if rev else b[c - 1:c]
        b_ends.append(b_end)
        q_ins.append((q * jnp.exp2(b)).astype(BF16))
        k_ends.append((k * jnp.exp2(b_end - b)).astype(BF16))

    sxs = [_dot(e_all, rexp) for e_all in e_alls]
    s_ods = [jnp.concatenate([_dot_nt(qm, ks) for qm, ks in zip(qm_d, ks_d)], axis=0).astype(BF16)
             for qm_d, ks_d in zip(qms, kss)]
    upds = [_dot_tn(w[2].astype(BF16), k_end) for k_end, w in zip(k_ends, work)]

    ress = [_dot(s_od, w[2].astype(BF16)) for s_od, w in zip(s_ods, work)]

    st_in, st_out, o_ins = [], [], []
    for d, w in enumerate(work):
        st = st_out[w[4]] if isinstance(w[4], int) else w[4]
        st_in.append(st)
        st_out.append(st * jnp.exp2(b_ends[d]) + jnp.where(bdmask, upds[d], 0.0))
        o_ins.append(_dot_nt(q_ins[d], st.astype(BF16)))

    outs = []
    for d, (q, k, v, a, st, rev) in enumerate(work):
        acc = [jnp.zeros((8, GROUP_W), F32) for _ in range(qk_tiles)]
        v32 = v.astype(F32)
        off = 0
        for j in range(c):
            t0, t1 = spans_all[d][j]
            vj = v32[j:j + 1]
            for t in range(t0, t1):
                acc[t] = acc[t] + sxs[d][off:off + 8] * vj
                off += 8
        for n, blk in enumerate(range(n_sb - 1) if rev else range(1, n_sb)):
            o_blk = jnp.zeros((sb, GROUP_W), F32)
            for h in range(N_H):
                r0 = (n * N_H + h) * sb
                o_blk = o_blk + jnp.where((lane_o // HEAD_DIM) == h, ress[d][r0:r0 + sb], 0.0)
            for tt in range(tiles_per_sb):
                t = blk * tiles_per_sb + tt
                acc[t] = acc[t] + o_blk[8 * tt:8 * tt + 8]
        o = jnp.concatenate(acc, axis=0) + o_ins[d]
        outs.append((o, st_out[d]))
    return outs


def _gla_kernel(qc_ref, kc_ref, vc_ref, afc_ref, abc_ref, ql_ref, kl_ref, vl_ref, afl_ref, abl_ref,
                oc_ref, ol_ref, sf_ref, sb_ref):
    c = GLA_CHUNK
    ri, ci = _iota((c, c), 0), _iota((c, c), 1)
    tri_f = jnp.where(ri >= ci, 1.0, 0.0).astype(F32)
    tri_b = jnp.where(ri <= ci, 1.0, 0.0).astype(F32)
    rexp = jnp.where((_iota((128, GROUP_W), 0) // GLA_DK) == (_iota((128, GROUP_W), 1) // HEAD_DIM),
                     1.0, 0.0).astype(BF16)
    bdmask = (_iota((GROUP_W, 128), 0) // HEAD_DIM) == (_iota((GROUP_W, 128), 1) // GLA_DK)
    rowid = _iota((c, 1), 0)
    consts = (tri_f, tri_b, rexp, bdmask, rowid)

    sf_ref[...] = jnp.zeros_like(sf_ref)
    sb_ref[...] = jnp.zeros_like(sb_ref)
    oc_ref[...] = jnp.zeros_like(oc_ref)
    ol_ref[...] = jnp.zeros_like(ol_ref)

    def sweep(q_ref, k_ref, v_ref, af_ref, ab_ref, o_ref):
        n = q_ref.shape[1] // c

        assert n % 2 == 0

        def body(i, carry):
            rows = [pl.ds(pl.multiple_of(idx * c, c), c)
                    for idx in (2 * i, n - 1 - 2 * i, 2 * i + 1, n - 2 - 2 * i)]
            plan = ((af_ref, sf_ref[...], False), (ab_ref, sb_ref[...], True), (af_ref, 0, False), (ab_ref, 1, True))
            work = [(q_ref[0, r, :], k_ref[0, r, :], v_ref[0, r, :], a_ref[0, r, :], st, rev)
                    for r, (a_ref, st, rev) in zip(rows, plan)]
            outs = _gla_chunks(work, consts)
            sf_ref[...] = outs[2][1]
            sb_ref[...] = outs[3][1]
            for r, (o, _) in zip(rows, outs):
                o_ref[0, r, :] += o
            return carry

        lax.fori_loop(0, n // 2, body, 0)

    sweep(qc_ref, kc_ref, vc_ref, afc_ref, abc_ref, oc_ref)
    sweep(ql_ref, kl_ref, vl_ref, afl_ref, abl_ref, ol_ref)


def _seq_spec(arr):
    return pl.BlockSpec((1,) + arr.shape[1:], lambda i: (i, 0, 0))


def _gla_call(pc, plat):
    b = pc["gq"].shape[0]
    args = [pc["gq"], pc["gk"], pc["gv"], pc["gaf"], pc["gab"],
            plat["gq"], plat["gk"], plat["gv"], plat["gaf"], plat["gab"]]
    lc, ll = pc["gq"].shape[1], plat["gq"].shape[1]
    return pl.pallas_call(
        _gla_kernel,
        grid=(b,),
        in_specs=[_seq_spec(a) for a in args],
        out_specs=[pl.BlockSpec((1, lc, GROUP_W), lambda i: (i, 0, 0)),
                   pl.BlockSpec((1, ll, GROUP_W), lambda i: (i, 0, 0))],
        out_shape=[jax.ShapeDtypeStruct((b, lc, GROUP_W), F32), jax.ShapeDtypeStruct((b, ll, GROUP_W), F32)],
        scratch_shapes=[pltpu.VMEM((GROUP_W, 128), F32), pltpu.VMEM((GROUP_W, 128), F32)],
        compiler_params=_params("parallel"),
        name="gla_scan",
    )(*args)


def _ret_kernel(lg_ref, qc_ref, kc_ref, vc_ref, ql_ref, kl_ref, vl_ref, oc_ref, ol_ref,
                sf_ref, sb_ref, dm_ref):
    c = RET_CHUNK
    lgam = _log_sigmoid(lg_ref[...])
    ri, ci = _iota((c, c), 0), _iota((c, c), 1)
    pos = _iota((c, 128), 0).astype(F32)
    dq, dk, dc = [], [], []
    for d in range(2):
        lg = lgam[d:d + 1]
        dist = (ri - ci) if d == 0 else (ci - ri)
        for h in range(N_H):
            lg_h = lg[:, h * RET_DK:h * RET_DK + 1]
            dm_ref[d, h] = jnp.exp(jnp.where(dist >= 0, dist.astype(F32) * lg_h, NEG_BIG))
        dq.append(jnp.exp(((pos + 1.0) if d == 0 else (c - pos)) * lg))
        dk.append(jnp.exp(((c - 1.0 - pos) if d == 0 else pos) * lg))
        dc.append(jnp.exp(c * lg))
    bdmask = (_iota((GROUP_W, 128), 0) // HEAD_DIM) == (_iota((GROUP_W, 128), 1) // RET_DK)
    lane_q = _iota((1, 128), 1)
    lane_o = _iota((1, GROUP_W), 1)

    sf_ref[...] = jnp.zeros_like(sf_ref)
    sb_ref[...] = jnp.zeros_like(sb_ref)
    oc_ref[...] = jnp.zeros_like(oc_ref)
    ol_ref[...] = jnp.zeros_like(ol_ref)

    def chunk(q, k, v, st_ref, d):
        st = st_ref[...]
        k16, v16 = k.astype(BF16), v.astype(BF16)
        o = _dot_nt((q * dq[d]).astype(BF16), st.astype(BF16))
        for h in range(N_H):
            qh = jnp.where((lane_q // RET_DK) == h, q, 0.0).astype(BF16)
            s = _dot_nt(qh, k16) * dm_ref[d, h]
            o = o + jnp.where((lane_o // HEAD_DIM) == h, _dot(s.astype(BF16), v16), 0.0)
        upd = _dot_tn(v16, (k * dk[d]).astype(BF16))
        st_ref[...] = st * dc[d] + jnp.where(bdmask, upd, 0.0)
        return o

    def sweep(q_ref, k_ref, v_ref, o_ref):
        n = q_ref.shape[1] // c

        def body(i, carry):
            for d, idx, st_ref in ((0, i, sf_ref), (1, n - 1 - i, sb_ref)):
                rows = pl.ds(pl.multiple_of(idx * c, c), c)
                o_ref[0, rows, :] += chunk(q_ref[0, rows, :], k_ref[0, rows, :], v_ref[0, rows, :], st_ref, d)
            return carry

        lax.fori_loop(0, n, body, 0)

    sweep(qc_ref, kc_ref, vc_ref, oc_ref)
    sweep(ql_ref, kl_ref, vl_ref, ol_ref)


def _ret_call(lg, pc, plat):
    b = pc["rq"].shape[0]
    args = [pc["rq"], pc["rk"], pc["rv"], plat["rq"], plat["rk"], plat["rv"]]
    lc, ll = pc["rq"].shape[1], plat["rq"].shape[1]
    return pl.pallas_call(
        _ret_kernel,
        grid=(b,),
        in_specs=[_const_spec(lg.shape)] + [_seq_spec(a) for a in args],
        out_specs=[pl.BlockSpec((1, lc, GROUP_W), lambda i: (i, 0, 0)),
                   pl.BlockSpec((1, ll, GROUP_W), lambda i: (i, 0, 0))],
        out_shape=[jax.ShapeDtypeStruct((b, lc, GROUP_W), F32), jax.ShapeDtypeStruct((b, ll, GROUP_W), F32)],
        scratch_shapes=[pltpu.VMEM((GROUP_W, 128), F32), pltpu.VMEM((GROUP_W, 128), F32),
                        pltpu.VMEM((2, N_H, RET_CHUNK, RET_CHUNK), F32)],
        compiler_params=_params("parallel"),
        name="ret_scan",
    )(lg, *args)


def _mix_out_kernel(x_ref, mod_ref, ng_ref, do_ref, mo_ref, go_ref, gr_ref, ro_ref, rg_ref,
                    gg_ref, rgn_ref, wo_ref, x1_ref, h2_ref):
    gmat = _group_mean_matrix(GROUP_W, HEAD_DIM)
    gl = _group_norm(go_ref[0], gg_ref[...], gmat) * _silu(gr_ref[0].astype(F32))
    rt = _group_norm(ro_ref[0], rgn_ref[...], gmat) * _silu(rg_ref[0].astype(F32))
    ml = (_dot(do_ref[0], wo_ref[0:256]) + _dot(mo_ref[0], wo_ref[256:512])
          + _dot(gl.astype(BF16), wo_ref[512:768]) + _dot(rt.astype(BF16), wo_ref[768:1024]))
    x1 = x_ref[0] + mod_ref[0, 2:3] * _rms(ml, ng_ref[1:2])
    x1_ref[0] = x1
    h2_ref[0] = (_rms(x1, ng_ref[2:3]) * (1.0 + mod_ref[0, 4:5]) + mod_ref[0, 3:4]).astype(BF16)


def _mod_spec(mod, d):
    if mod.shape[0] > 1:
        return pl.BlockSpec((1, 6, d), lambda i, j: (i, 0, 0))
    return pl.BlockSpec((1, 6, d), lambda i, j: (0, 0, 0))


def _mix_out_call(x, mod, ng, d_o, m_o, g_o, g_r, r_o, r_g, lw, tm):
    b, n, d = x.shape

    def tok(w):
        return pl.BlockSpec((1, tm, w), lambda i, j: (i, j, 0))

    return pl.pallas_call(
        _mix_out_kernel,
        grid=(b, n // tm),
        in_specs=[tok(d), _mod_spec(mod, d), _const_spec(ng.shape)] + [tok(GROUP_W)] * 6
                 + [_const_spec(lw["gla_g"].shape), _const_spec(lw["ret_g"].shape), _const_spec(lw["w_out"].shape)],
        out_specs=[tok(d), tok(d)],
        out_shape=[jax.ShapeDtypeStruct((b, n, d), F32), jax.ShapeDtypeStruct((b, n, d), BF16)],
        compiler_params=_params("parallel", "parallel"),
        name="mix_out",
    )(x, mod, ng, d_o, m_o, g_o, g_r, r_o, r_g, lw["gla_g"], lw["ret_g"], lw["w_out"])


def _ffn_kernel(h_ref, x1_ref, mod_ref, ng_ref, wg_ref, wu_ref, wo_ref, x2_ref):
    h = h_ref[0]
    a = (_silu(_dot(h, wg_ref[...])) * _dot(h, wu_ref[...])).astype(BF16)
    y = _dot(a, wo_ref[...])
    x2_ref[0] = x1_ref[0] + mod_ref[0, 5:6] * _rms(y, ng_ref[3:4])


def _ffn_call(h2, x1, mod, ng, lw, tm):
    b, n, d = x1.shape

    def tok(w):
        return pl.BlockSpec((1, tm, w), lambda i, j: (i, j, 0))

    return pl.pallas_call(
        _ffn_kernel,
        grid=(b, n // tm),
        in_specs=[tok(d), tok(d), _mod_spec(mod, d), _const_spec(ng.shape),
                  _const_spec(lw["ffn_g"].shape), _const_spec(lw["ffn_u"].shape), _const_spec(lw["ffn_o"].shape)],
        out_specs=tok(d),
        out_shape=jax.ShapeDtypeStruct((b, n, d), F32),
        compiler_params=_params("parallel", "parallel"),
        name="ffn",
    )(h2, x1, mod, ng, lw["ffn_g"], lw["ffn_u"], lw["ffn_o"])


def _pack_layer(l, w_in, diff_norm, mla_q_norm, mla_kv_norm, mla_w_uq, mla_w_ukv, gla_w_gate, gla_b_gate,
                gla_norm, ret_decay, ret_norm, w_out, ffn_w_in, ffn_w_out):
    d = w_in.shape[1]
    w = w_in[l]
    sizes = (256, 256, 256, 256, 128, 32, 128, 128, 256, 256, 16, 16, 128, 128, 256, 256)
    offs = np.concatenate([[0], np.cumsum(sizes)])
    seg = [w[:, offs[i]:offs[i + 1]] for i in range(len(sizes))]
    z = lambda n: jnp.zeros((d, n), w.dtype)
    misc = jnp.concatenate([seg[10], seg[11], z(32), seg[5], z(32)], axis=1)
    wm = jnp.concatenate(seg[0:5] + [misc] + seg[6:10] + seg[12:16], axis=1).astype(BF16)

    uq = mla_w_uq[l].reshape(-1, N_H, MLA_NOPE + MLA_ROPE)
    uq = jnp.pad(uq, ((0, 0), (0, 0), (0, MLA_SLOT - MLA_NOPE - MLA_ROPE))).reshape(-1, N_H * MLA_SLOT)
    ukv = mla_w_ukv[l].reshape(-1, N_H, MLA_NOPE + HEAD_DIM)
    uk = jnp.pad(ukv[:, :, :MLA_NOPE], ((0, 0), (0, 0), (0, MLA_SLOT - MLA_NOPE))).reshape(-1, N_H * MLA_SLOT)
    uv = ukv[:, :, MLA_NOPE:].reshape(-1, GROUP_W)

    wgf = jnp.zeros((128, 128), F32).at[0:GLA_GATE_RANK].set(gla_w_gate[l, 0]).astype(BF16)
    wgb = jnp.zeros((128, 128), F32).at[GLA_GATE_RANK:2 * GLA_GATE_RANK].set(gla_w_gate[l, 1]).astype(BF16)
    hid = ffn_w_out.shape[1]
    return {
        "wm": wm, "qg": mla_q_norm[l][None], "kvg": mla_kv_norm[l][None],
        "wuq": uq.astype(BF16), "wuk": uk.astype(BF16), "wuv": uv.astype(BF16),
        "wgf": wgf, "wgb": wgb, "bg": gla_b_gate[l],
        "diff_g": jnp.tile(diff_norm[l], N_H)[None], "gla_g": jnp.tile(gla_norm[l], N_H)[None],
        "ret_g": jnp.tile(ret_norm[l], N_H)[None],
        "ret_lg": jnp.repeat(ret_decay[l], RET_DK, axis=1),
        "w_out": w_out[l].astype(BF16),
        "ffn_g": ffn_w_in[l][:, :hid].astype(BF16), "ffn_u": ffn_w_in[l][:, hid:].astype(BF16),
        "ffn_o": ffn_w_out[l].astype(BF16),
    }


def _rope_tables(seq):
    rows = seq // GRID_W
    row = jnp.repeat(jnp.arange(rows, dtype=F32), GRID_W)
    col = jnp.tile(jnp.arange(GRID_W, dtype=F32), rows)
    n_freq = ROT_DIM // 4
    freqs = ROPE_THETA ** (-jnp.arange(n_freq, dtype=F32) / n_freq)
    ang = jnp.concatenate([row[:, None] * freqs, col[:, None] * freqs], axis=-1)
    cos, sin = jnp.cos(ang), jnp.sin(ang)
    c32 = jnp.concatenate([cos, cos], axis=-1)
    s32 = jnp.concatenate([-sin, sin], axis=-1)
    one, zero = jnp.ones((seq, 1), F32), jnp.zeros((seq, 1), F32)
    slot_c = jnp.concatenate([jnp.tile(one, (1, MLA_NOPE)), c32, jnp.tile(one, (1, MLA_SLOT - MLA_NOPE - MLA_ROPE))], -1)
    slot_s = jnp.concatenate([jnp.tile(zero, (1, MLA_NOPE)), s32, jnp.tile(zero, (1, MLA_SLOT - MLA_NOPE - MLA_ROPE))], -1)
    return (jnp.tile(c32, (1, 256 // ROT_DIM)), jnp.tile(s32, (1, 256 // ROT_DIM)),
            jnp.tile(slot_c, (1, N_H)), jnp.tile(slot_s, (1, N_H)))


def _tile(n, want):
    t = min(n, want)
    while n % t:
        t //= 2
    return t


def kernel(x, c, ctx, c_ctx, ada_w, ada_b, norm_g, w_in, diff_lam, diff_norm, mla_q_norm, mla_kv_norm,
           mla_w_uq, mla_w_ukv, gla_w_gate, gla_b_gate, gla_norm, ret_decay, ret_norm, w_out, ffn_w_in, ffn_w_out):
    b, seq, d = x.shape
    lc = ctx.shape[1]
    depth = ada_w.shape[0]
    rope_tabs = _rope_tables(seq)

    pad = (-(b + 1)) % 8
    s_all = jnp.concatenate([c, c_ctx[None], jnp.zeros((pad, d), F32)], axis=0)
    mods = _ada_call(s_all, ada_w, ada_b)

    tm_l, tm_c = _tile(seq, 512), _tile(lc, 256)
    tq_c = _tile(lc, 256)
    tq_d, tk_d = _tile(seq, 256), _tile(seq, 1024)
    tq_m, tk_m = _tile(seq, 512), _tile(seq, 1024)

    xl, xc = x, ctx
    for l in range(depth):
        with_ctx = l < depth - 1
        lambda_init = 0.8 - 0.6 * math.exp(-0.3 * l)
        lw = _pack_layer(l, w_in, diff_norm, mla_q_norm, mla_kv_norm, mla_w_uq, mla_w_ukv, gla_w_gate,
                         gla_b_gate, gla_norm, ret_decay, ret_norm, w_out, ffn_w_in, ffn_w_out)
        mod_l = mods[l, :b].reshape(b, 6, d)
        mod_c = mods[l, b:b + 1].reshape(1, 6, d)
        ng = norm_g[l]

        pc = _proj_call(xc, mod_c, ng, lw, None, tm_c)
        pt = _proj_call(xl, mod_l, ng, lw, rope_tabs, tm_l)

        dargs = (diff_lam[l], lw["diff_g"], lambda_init)
        d_l = _attn_call(pt["dq"], pc["dk"], pc["dv"], pt["dk"], pt["dv"], dargs, tq_d, tk_d, "diff_attn")
        m_l = _attn_call(pt["mq"], pc["mk"], pc["mv"], pt["mk"], pt["mv"], None, tq_m, tk_m, "mla_attn")
        g_c, g_l = _gla_call(pc, pt)
        r_c, r_l = _ret_call(lw["ret_lg"], pc, pt)

        x1, h2 = _mix_out_call(xl, mod_l, ng, d_l, m_l, g_l, pt["gr"], r_l, pt["rg"], lw, tm_l)
        xl = _ffn_call(h2, x1, mod_l, ng, lw, tm_l)
        if with_ctx:
            d_c = _attn_call(pc["dq"], pc["dk"], pc["dv"], None, None, dargs, tq_c, tk_d, "diff_attn_ctx")
            m_c = _attn_call(pc["mq"], pc["mk"], pc["mv"], None, None, None, tq_c, tk_m, "mla_attn_ctx")
            x1c, h2c = _mix_out_call(xc, mod_c, ng, d_c, m_c, g_c, pc["gr"], r_c, pc["rg"], lw, tm_c)
            xc = _ffn_call(h2c, x1c, mod_c, ng, lw, tm_c)
    return xl
```

```python
import functools
import math

import jax
import jax.numpy as jnp
import numpy as np
from jax import lax
from jax.experimental import pallas as pl
from jax.experimental.pallas import tpu as pltpu

GRID_W = 64
N_H = 4
HEAD_DIM = 64
GROUP_W = N_H * HEAD_DIM
ROT_DIM = 32
ROPE_THETA = 10000.0
DIFF_QK = 32
MLA_NOPE = 64
MLA_ROPE = 32
MLA_SLOT = 128
V_SLOT = 128
V_ONES = HEAD_DIM
LOG2E = math.log2(math.e)
GLA_DK = 32
GLA_GATE_RANK = 16
GLA_GATE_NORM = 16.0
GLA_CHUNK = 64
GLA_SUB = 16
RET_DK = 32
RET_CHUNK = 256
RMS_EPS = 1e-6
NEG_BIG = -1e30

VMEM_LIMIT_BYTES = 48 * 1024 * 1024
F32 = jnp.float32
BF16 = jnp.bfloat16
HIGHEST = lax.Precision.HIGHEST

C_DQ, C_DK, C_DV, C_CQ, C_CKV, C_MISC = 0, 256, 512, 768, 1024, 1152
C_GQ, C_GK, C_GV, C_GR, C_RQ, C_RK, C_RV, C_RG, C_END = 1280, 1408, 1536, 1792, 2048, 2176, 2304, 2560, 2816
MISC_KR = 64


def _dot(a, b, precision=None):
    return jnp.dot(a, b, preferred_element_type=F32, precision=precision)


def _dot_nt(a, b):
    return lax.dot_general(a, b, (((1,), (1,)), ((), ())), preferred_element_type=F32)


def _dot_tn(a, b):
    return lax.dot_general(a, b, (((0,), (0,)), ((), ())), preferred_element_type=F32)


def _rms(x, g):
    ms = jnp.mean(x * x, axis=-1, keepdims=True)
    return x * lax.rsqrt(ms + RMS_EPS) * g


def _silu(x):
    return x * (1.0 / (1.0 + jnp.exp(-x)))


def _log_sigmoid(x):
    return jnp.minimum(x, 0.0) - jnp.log1p(jnp.exp(-jnp.abs(x)))


def _iota(shape, dim):
    return lax.broadcasted_iota(jnp.int32, shape, dim)


def _group_mean_matrix(width, group):
    same = (_iota((width, width), 0) // group) == (_iota((width, width), 1) // group)
    return jnp.where(same, 1.0 / group, 0.0).astype(BF16)


def _group_norm(o, g, gmat):
    ms = _dot((o * o).astype(BF16), gmat)
    return o * lax.rsqrt(ms + RMS_EPS) * g


def _rope(t, cs, sn):
    w = t.shape[-1]
    lo = (_iota((1, w), 1) % ROT_DIM) < (ROT_DIM // 2)
    swapped = jnp.where(lo, pltpu.roll(t, w - ROT_DIM // 2, 1), pltpu.roll(t, ROT_DIM // 2, 1))
    return t * cs + swapped * sn


def _const_spec(shape):
    nd = len(shape)
    return pl.BlockSpec(shape, lambda *_: (0,) * nd)


def _params(*sem):
    return pltpu.CompilerParams(dimension_semantics=sem, vmem_limit_bytes=VMEM_LIMIT_BYTES)


def _ada_kernel(s_ref, w_ref, b_ref, o_ref):
    s = _silu(s_ref[...])
    o_ref[0] = _dot(s, w_ref[0], precision=HIGHEST) + b_ref[0]


def _ada_call(s_all, ada_w, ada_b):
    depth, d, n = ada_w.shape
    rows = s_all.shape[0]
    tn = 1024
    return pl.pallas_call(
        _ada_kernel,
        grid=(depth, n // tn),
        in_specs=[pl.BlockSpec((rows, d), lambda l, j: (0, 0)),
                  pl.BlockSpec((1, d, tn), lambda l, j: (l, 0, j)),
                  pl.BlockSpec((1, 1, tn), lambda l, j: (l, 0, j))],
        out_specs=pl.BlockSpec((1, rows, tn), lambda l, j: (l, 0, j)),
        out_shape=jax.ShapeDtypeStruct((depth, rows, n), F32),
        compiler_params=_params("parallel", "parallel"),
        name="ada_mod",
    )(s_all, ada_w, ada_b.reshape(depth, 1, n))


def _proj_kernel(*refs, rotate):
    (x_ref, mod_ref, ng_ref, wm_ref, qg_ref, kvg_ref, wuq_ref, wuk_ref, wuv_ref,
     wgf_ref, wgb_ref, bg_ref) = refs[:12]
    refs = refs[12:]
    if rotate:
        cs256_ref, sn256_ref, cs512_ref, sn512_ref = refs[:4]
        refs = refs[4:]
    (dq_ref, dk_ref, dv_ref, mq_ref, mk_ref, mv_ref, gq_ref, gk_ref, gv_ref, gr_ref,
     gaf_ref, gab_ref, rq_ref, rk_ref, rv_ref, rg_ref) = refs

    x = x_ref[0]
    h = (_rms(x, ng_ref[0:1]) * (1.0 + mod_ref[0, 1:2]) + mod_ref[0, 0:1]).astype(BF16)

    def proj(lo, hi):
        return _dot(h, wm_ref[:, lo:hi])

    def rope256(t):
        return _rope(t, cs256_ref[...], sn256_ref[...]) if rotate else t

    def rope128(t):
        return _rope(t, cs256_ref[:, :128], sn256_ref[:, :128]) if rotate else t

    def rope512(t):
        return _rope(t, cs512_ref[...], sn512_ref[...]) if rotate else t

    lane = _iota((1, 128), 1)
    ones_lane = jnp.where(lane == V_ONES, 1.0, 0.0)

    def store_values(v_ref, v):
        rolled = pltpu.roll(v, GROUP_W - HEAD_DIM, 1)
        for hd in range(N_H):
            src = v if hd % 2 == 0 else rolled
            blk = src[:, (hd // 2) * 128:(hd // 2 + 1) * 128]
            v_ref[0, hd] = jnp.where(lane < HEAD_DIM, blk, ones_lane).astype(BF16)

    dq_ref[0] = (rope256(proj(C_DQ, C_DK)) * (DIFF_QK ** -0.5 * LOG2E)).astype(BF16)
    dk_ref[0] = rope256(proj(C_DK, C_DV)).astype(BF16)
    store_values(dv_ref, proj(C_DV, C_CQ))

    qn = _rms(proj(C_CQ, C_CKV), qg_ref[...]).astype(BF16)
    q512 = (rope512(_dot(qn, wuq_ref[...])) * ((MLA_NOPE + MLA_ROPE) ** -0.5 * LOG2E)).astype(BF16)
    kvn = _rms(proj(C_CKV, C_MISC), kvg_ref[...]).astype(BF16)
    misc = proj(C_MISC, C_GQ)
    kr = jnp.where((lane >= MISC_KR) & (lane < MISC_KR + MLA_ROPE), misc, 0.0)
    k512 = rope512(_dot(kvn, wuk_ref[...]) + jnp.concatenate([kr] * N_H, axis=1)).astype(BF16)
    for hd in range(N_H):
        mq_ref[0, hd] = q512[:, hd * MLA_SLOT:(hd + 1) * MLA_SLOT]
        mk_ref[0, hd] = k512[:, hd * MLA_SLOT:(hd + 1) * MLA_SLOT]
    store_values(mv_ref, _dot(kvn, wuv_ref[...]))

    gq_ref[0] = proj(C_GQ, C_GK) * GLA_DK ** -0.5
    gk_ref[0] = proj(C_GK, C_GV)
    gv_ref[0] = proj(C_GV, C_GR).astype(BF16)
    gr_ref[0] = proj(C_GR, C_RQ).astype(BF16)
    misc16 = misc.astype(BF16)
    gaf_ref[0] = _log_sigmoid(_dot(misc16, wgf_ref[...]) + bg_ref[0:1]) / GLA_GATE_NORM
    gab_ref[0] = _log_sigmoid(_dot(misc16, wgb_ref[...]) + bg_ref[1:2]) / GLA_GATE_NORM

    rq_ref[0] = rope128(proj(C_RQ, C_RK))
    rk_ref[0] = rope128(proj(C_RK, C_RV) * RET_DK ** -0.5)
    rv_ref[0] = proj(C_RV, C_RG).astype(BF16)
    rg_ref[0] = proj(C_RG, C_END).astype(BF16)


_HEAD_MAJOR = ("mq", "mk", "dv", "mv")
_PROJ_OUT = (("dq", 256, BF16), ("dk", 256, BF16), ("dv", V_SLOT, BF16),
             ("mq", MLA_SLOT, BF16), ("mk", MLA_SLOT, BF16), ("mv", V_SLOT, BF16),
             ("gq", 128, F32), ("gk", 128, F32), ("gv", 256, BF16), ("gr", 256, BF16),
             ("gaf", 128, F32), ("gab", 128, F32),
             ("rq", 128, F32), ("rk", 128, F32), ("rv", 256, BF16), ("rg", 256, BF16))


def _proj_call(x, mod, ng, lw, rope_tabs, tm):
    b, n, d = x.shape
    rotate = rope_tabs is not None
    per_batch_mod = mod.shape[0] > 1
    weights = (lw["wm"], lw["qg"], lw["kvg"], lw["wuq"], lw["wuk"], lw["wuv"], lw["wgf"], lw["wgb"], lw["bg"])
    in_specs = [pl.BlockSpec((1, tm, d), lambda i, j: (i, j, 0)),
                pl.BlockSpec((1, 6, d), (lambda i, j: (i, 0, 0)) if per_batch_mod else (lambda i, j: (0, 0, 0))),
                _const_spec(ng.shape)]
    in_specs += [_const_spec(w.shape) for w in weights]
    args = [x, mod, ng, *weights]
    if rotate:
        in_specs += [pl.BlockSpec((tm, t.shape[1]), lambda i, j: (j, 0)) for t in rope_tabs]
        args += list(rope_tabs)
    out_specs, out_shape = [], []
    for name, w, dt in _PROJ_OUT:
        if name in _HEAD_MAJOR:
            out_specs.append(pl.BlockSpec((1, N_H, tm, w), lambda i, j: (i, 0, j, 0)))
            out_shape.append(jax.ShapeDtypeStruct((b, N_H, n, w), dt))
        else:
            out_specs.append(pl.BlockSpec((1, tm, w), lambda i, j: (i, j, 0)))
            out_shape.append(jax.ShapeDtypeStruct((b, n, w), dt))
    outs = pl.pallas_call(
        functools.partial(_proj_kernel, rotate=rotate),
        grid=(b, n // tm),
        in_specs=in_specs,
        out_specs=out_specs,
        out_shape=out_shape,
        compiler_params=_params("parallel", "parallel"),
        name="in_proj_rot" if rotate else "in_proj",
    )(*args)
    return {name: o for (name, _, _), o in zip(_PROJ_OUT, outs)}


def _attn_kernel(*refs, diff, has_lat, tk, lambda_init):
    q_ref, kc_ref, vc_ref = refs[:3]
    refs = refs[3:]
    if has_lat:
        kl_ref, vl_ref = refs[:2]
        refs = refs[2:]
    if diff:
        lam_ref, g_ref = refs[:2]
        refs = refs[2:]
    if diff:
        o_ref, acc_ref, qst_ref = refs
    else:
        o_ref, acc_ref = refs
    tq = o_ref.shape[1]
    n_maps = 2 if diff else 1

    if diff:
        q = q_ref[0]
        lane_q = _iota((1, q.shape[-1]), 1)
        for i in range(2 * N_H):
            qst_ref[i * tq:(i + 1) * tq, :] = jnp.where((lane_q // DIFF_QK) == i, q, jnp.zeros_like(q))

    def scores(k_ref, rows):
        if diff:
            return [(_dot_nt(qst_ref[...], k_ref[0, rows, :]), 0)]
        return [(_dot_nt(q_ref[0, h], k_ref[0, h, rows, :]), h * tq) for h in range(N_H)]

    def softmax_pv(blocks, v_ref, rows, ms):
        new = []
        for g, (s, r0) in enumerate(blocks):
            mx = jnp.max(s, axis=-1, keepdims=True)
            if ms is not None:
                mx = jnp.maximum(ms[g], mx)
                alpha = jnp.exp2(ms[g] - mx)
            p = jnp.exp2(s - mx).astype(BF16)
            hrows = n_maps * tq
            for j in range(s.shape[0] // hrows):
                h = (r0 + j * hrows) // hrows
                pv = _dot(p[j * hrows:(j + 1) * hrows], v_ref[0, h, rows, :])
                arows = slice(r0 + j * hrows, r0 + (j + 1) * hrows)
                if ms is None:
                    acc_ref[arows, :] = pv
                else:
                    acc_ref[arows, :] = alpha[j * hrows:(j + 1) * hrows] * acc_ref[arows, :] + pv
            new.append(mx)
        return tuple(new)

    ctx_scores = scores(kc_ref, slice(None))
    if has_lat:
        nxt = scores(kl_ref, slice(0, tk))
    ms = softmax_pv(ctx_scores, vc_ref, slice(None), None)
    if has_lat:
        n_blocks = kl_ref.shape[-2] // tk
        for c in range(n_blocks):
            cur = nxt
            if c + 1 < n_blocks:
                nxt = scores(kl_ref, slice((c + 1) * tk, (c + 2) * tk))
            ms = softmax_pv(cur, vl_ref, slice(c * tk, (c + 1) * tk), ms)

    def head_out(i):
        a = acc_ref[i * tq:(i + 1) * tq, :]
        return a[:, :HEAD_DIM] * (1.0 / a[:, V_ONES:V_ONES + 1])

    if diff:
        lv = lam_ref[...]
        lam = (jnp.exp(jnp.sum(lv[0:1] * lv[1:2], axis=-1, keepdims=True))
               - jnp.exp(jnp.sum(lv[2:3] * lv[3:4], axis=-1, keepdims=True)) + lambda_init)
        out = jnp.concatenate([head_out(2 * h) - lam * head_out(2 * h + 1) for h in range(N_H)], axis=1)
        out = _group_norm(out, g_ref[...], _group_mean_matrix(GROUP_W, HEAD_DIM)) * (1.0 - lambda_init)
    else:
        out = jnp.concatenate([head_out(h) for h in range(N_H)], axis=1)
    o_ref[0] = out.astype(o_ref.dtype)


def _attn_call(q, kc, vc, kl, vl, diff_args, tq, tk, name):
    has_lat = kl is not None
    diff = diff_args is not None
    b, lq = q.shape[0], q.shape[-2]

    def spec(arr, rows, tiled):
        if arr.ndim == 4:
            return pl.BlockSpec((1, N_H, rows, arr.shape[3]), (lambda i, j: (i, 0, j, 0)) if tiled else (lambda i, j: (i, 0, 0, 0)))
        return pl.BlockSpec((1, rows, arr.shape[2]), (lambda i, j: (i, j, 0)) if tiled else (lambda i, j: (i, 0, 0)))

    in_specs = [spec(q, tq, True), spec(kc, kc.shape[-2], False), spec(vc, vc.shape[-2], False)]
    args = [q, kc, vc]
    if has_lat:
        in_specs += [spec(kl, kl.shape[-2], False), spec(vl, vl.shape[-2], False)]
        args += [kl, vl]
    lambda_init = 0.0
    if diff:
        lam_vecs, sub_g, lambda_init = diff_args
        in_specs += [_const_spec(lam_vecs.shape), _const_spec(sub_g.shape)]
        args += [lam_vecs, sub_g]
    return pl.pallas_call(
        functools.partial(_attn_kernel, diff=diff, has_lat=has_lat, tk=tk, lambda_init=lambda_init),
        grid=(b, lq // tq),
        in_specs=in_specs,
        out_specs=pl.BlockSpec((1, tq, GROUP_W), lambda i, j: (i, j, 0)),
        out_shape=jax.ShapeDtypeStruct((b, lq, GROUP_W), BF16),
        scratch_shapes=[pltpu.VMEM((N_H * (2 if diff else 1) * tq, V_SLOT), F32)]
                       + ([pltpu.VMEM((2 * N_H * tq, q.shape[-1]), BF16)] if diff else []),
        compiler_params=_params("parallel", "parallel"),
        name=name,
    )(*args)


def _gla_chunks(work, consts):
    tri_f, tri_b, rexp, bdmask, rowid = consts
    c, sb = GLA_CHUNK, GLA_SUB
    qk_tiles, tiles_per_sb, n_sb = c // 8, sb // 8, c // sb
    lane_k = _iota((1, 128), 1)
    lane_o = _iota((1, GROUP_W), 1)

    bs = [_dot(tri_b if rev else tri_f, a, precision=HIGHEST) * LOG2E for (_, _, _, a, _, rev) in work]

    e_alls, spans_all, qms, kss, q_ins, k_ends, b_ends = [], [], [], [], [], [], []
    for (q, k, v, a, st, rev), b in zip(work, bs):
        pieces, spans = [], []
        for j in range(c):
            t, blk = j // 8, j // sb
            t0, t1 = (blk * tiles_per_sb, t + 1) if rev else (t, (blk + 1) * tiles_per_sb)
            for tt in range(t0, t1):
                rows = slice(8 * tt, 8 * tt + 8)
                arg = b[rows] - b[j:j + 1]
                if tt == t:
                    keep = (rowid[rows] <= j) if rev else (rowid[rows] >= j)
                    arg = jnp.where(keep, arg, NEG_BIG)
                pieces.append(jnp.exp2(arg) * (q[rows] * k[j:j + 1]))
            spans.append((t0, t1))
        e_alls.append(jnp.concatenate(pieces, axis=0).astype(BF16))
        spans_all.append(spans)
        qm_d, ks_d = [], []
        for blk in (range(n_sb - 1) if rev else range(1, n_sb)):
            r0 = blk * sb
            if rev:
                ref_row, seen = b[r0 + sb:r0 + sb + 1], rowid >= r0 + sb
            else:
                ref_row, seen = b[r0 - 1:r0], rowid < r0
            qs = q[r0:r0 + sb] * jnp.exp2(b[r0:r0 + sb] - ref_row)
            ks_d.append((k * jnp.exp2(jnp.where(seen, ref_row - b, NEG_BIG))).astype(BF16))
            qm_d.append(jnp.concatenate([jnp.where((lane_k // GLA_DK) == h, qs, 0.0) for h in range(N_H)],
                                        axis=0).astype(BF16))
        qms.append(qm_d)
        kss.append(ks_d)
        b_end = b[0:1] if rev else b[c - 1:c]
        b_ends.append(b_end)
        q_ins.append((q * jnp.exp2(b)).astype(BF16))
        k_ends.append((k * jnp.exp2(b_end - b)).astype(BF16))

    sxs = [_dot(e_all, rexp) for e_all in e_alls]
    s_ods = [jnp.concatenate([_dot_nt(qm, ks) for qm, ks in zip(qm_d, ks_d)], axis=0).astype(BF16)
             for qm_d, ks_d in zip(qms, kss)]
    upds = [_dot_tn(w[2].astype(BF16), k_end) for k_end, w in zip(k_ends, work)]

    ress = [_dot(s_od, w[2].astype(BF16)) for s_od, w in zip(s_ods, work)]

    st_in, st_out, o_ins = [], [], []
    for d, w in enumerate(work):
        st = st_out[w[4]] if isinstance(w[4], int) else w[4]
        st_in.append(st)
        st_out.append(st * jnp.exp2(b_ends[d]) + jnp.where(bdmask, upds[d], 0.0))
        o_ins.append(_dot_nt(q_ins[d], st.astype(BF16)))

    outs = []
    for d, (q, k, v, a, st, rev) in enumerate(work):
        acc = [jnp.zeros((8, GROUP_W), F32) for _ in range(qk_tiles)]
        v32 = v.astype(F32)
        off = 0
        for j in range(c):
            t0, t1 = spans_all[d][j]
            vj = v32[j:j + 1]
            for t in range(t0, t1):
                acc[t] = acc[t] + sxs[d][off:off + 8] * vj
                off += 8
        for n, blk in enumerate(range(n_sb - 1) if rev else range(1, n_sb)):
            o_blk = jnp.zeros((sb, GROUP_W), F32)
            for h in range(N_H):
                r0 = (n * N_H + h) * sb
                o_blk = o_blk + jnp.where((lane_o // HEAD_DIM) == h, ress[d][r0:r0 + sb], 0.0)
            for tt in range(tiles_per_sb):
                t = blk * tiles_per_sb + tt
                acc[t] = acc[t] + o_blk[8 * tt:8 * tt + 8]
        o = jnp.concatenate(acc, axis=0) + o_ins[d]
        outs.append((o, st_out[d]))
    return outs


def _gla_kernel(qc_ref, kc_ref, vc_ref, afc_ref, abc_ref, ql_ref, kl_ref, vl_ref, afl_ref, abl_ref,
                oc_ref, ol_ref, sf_ref, sb_ref):
    c = GLA_CHUNK
    ri, ci = _iota((c, c), 0), _iota((c, c), 1)
    tri_f = jnp.where(ri >= ci, 1.0, 0.0).astype(F32)
    tri_b = jnp.where(ri <= ci, 1.0, 0.0).astype(F32)
    rexp = jnp.where((_iota((128, GROUP_W), 0) // GLA_DK) == (_iota((128, GROUP_W), 1) // HEAD_DIM),
                     1.0, 0.0).astype(BF16)
    bdmask = (_iota((GROUP_W, 128), 0) // HEAD_DIM) == (_iota((GROUP_W, 128), 1) // GLA_DK)
    rowid = _iota((c, 1), 0)
    consts = (tri_f, tri_b, rexp, bdmask, rowid)

    sf_ref[...] = jnp.zeros_like(sf_ref)
    sb_ref[...] = jnp.zeros_like(sb_ref)
    oc_ref[...] = jnp.zeros_like(oc_ref)
    ol_ref[...] = jnp.zeros_like(ol_ref)

    def sweep(q_ref, k_ref, v_ref, af_ref, ab_ref, o_ref):
        n = q_ref.shape[1] // c

        assert n % 2 == 0

        def body(i, carry):
            rows = [pl.ds(pl.multiple_of(idx * c, c), c)
                    for idx in (2 * i, n - 1 - 2 * i, 2 * i + 1, n - 2 - 2 * i)]
            plan = ((af_ref, sf_ref[...], False), (ab_ref, sb_ref[...], True), (af_ref, 0, False), (ab_ref, 1, True))
            work = [(q_ref[0, r, :], k_ref[0, r, :], v_ref[0, r, :], a_ref[0, r, :], st, rev)
                    for r, (a_ref, st, rev) in zip(rows, plan)]
            outs = _gla_chunks(work, consts)
            sf_ref[...] = outs[2][1]
            sb_ref[...] = outs[3][1]
            for r, (o, _) in zip(rows, outs):
                o_ref[0, r, :] += o
            return carry

        lax.fori_loop(0, n // 2, body, 0)

    sweep(qc_ref, kc_ref, vc_ref, afc_ref, abc_ref, oc_ref)
    sweep(ql_ref, kl_ref, vl_ref, afl_ref, abl_ref, ol_ref)


def _seq_spec(arr):
    return pl.BlockSpec((1,) + arr.shape[1:], lambda i: (i, 0, 0))


def _gla_call(pc, plat):
    b = pc["gq"].shape[0]
    args = [pc["gq"], pc["gk"], pc["gv"], pc["gaf"], pc["gab"],
            plat["gq"], plat["gk"], plat["gv"], plat["gaf"], plat["gab"]]
    lc, ll = pc["gq"].shape[1], plat["gq"].shape[1]
    return pl.pallas_call(
        _gla_kernel,
        grid=(b,),
        in_specs=[_seq_spec(a) for a in args],
        out_specs=[pl.BlockSpec((1, lc, GROUP_W), lambda i: (i, 0, 0)),
                   pl.BlockSpec((1, ll, GROUP_W), lambda i: (i, 0, 0))],
        out_shape=[jax.ShapeDtypeStruct((b, lc, GROUP_W), F32), jax.ShapeDtypeStruct((b, ll, GROUP_W), F32)],
        scratch_shapes=[pltpu.VMEM((GROUP_W, 128), F32), pltpu.VMEM((GROUP_W, 128), F32)],
        compiler_params=_params("parallel"),
        name="gla_scan",
    )(*args)


def _ret_kernel(lg_ref, qc_ref, kc_ref, vc_ref, ql_ref, kl_ref, vl_ref, oc_ref, ol_ref,
                sf_ref, sb_ref, dm_ref):
    c = RET_CHUNK
    lgam = _log_sigmoid(lg_ref[...])
    ri, ci = _iota((c, c), 0), _iota((c, c), 1)
    pos = _iota((c, 128), 0).astype(F32)
    dq, dk, dc = [], [], []
    for d in range(2):
        lg = lgam[d:d + 1]
        dist = (ri - ci) if d == 0 else (ci - ri)
        for h in range(N_H):
            lg_h = lg[:, h * RET_DK:h * RET_DK + 1]
            dm_ref[d, h * c:(h + 1) * c, :] = jnp.exp(jnp.where(dist >= 0, dist.astype(F32) * lg_h, NEG_BIG))
        dq.append(jnp.exp(((pos + 1.0) if d == 0 else (c - pos)) * lg))
        dk.append(jnp.exp(((c - 1.0 - pos) if d == 0 else pos) * lg))
        dc.append(jnp.exp(c * lg))
    bdmask = (_iota((GROUP_W, 128), 0) // HEAD_DIM) == (_iota((GROUP_W, 128), 1) // RET_DK)
    lane_q = _iota((1, 128), 1)
    lane_o = _iota((1, GROUP_W), 1)

    sf_ref[...] = jnp.zeros_like(sf_ref)
    sb_ref[...] = jnp.zeros_like(sb_ref)
    oc_ref[...] = jnp.zeros_like(oc_ref)
    ol_ref[...] = jnp.zeros_like(ol_ref)

    def chunks(items):
        qst = [jnp.concatenate([jnp.where((lane_q // RET_DK) == h, q, 0.0) for h in range(N_H)], axis=0).astype(BF16)
               for q, _, _, _, _ in items]
        s = [_dot_nt(qs, k.astype(BF16)) for qs, (_, k, _, _, _) in zip(qst, items)]
        o_in = [_dot_nt((q * dq[d]).astype(BF16), st.astype(BF16)) for q, _, _, st, d in items]
        upd = [_dot_tn(v.astype(BF16), (k * dk[d]).astype(BF16)) for _, k, v, _, d in items]
        pv = [_dot((sd * dm_ref[d]).astype(BF16), v.astype(BF16))
              for sd, (_, _, v, _, d) in zip(s, items)]
        outs = []
        for n_i, (_, _, _, st, d) in enumerate(items):
            o = o_in[n_i]
            for h in range(N_H):
                o = o + jnp.where((lane_o // HEAD_DIM) == h, pv[n_i][h * c:(h + 1) * c], 0.0)
            outs.append((o, st * dc[d] + jnp.where(bdmask, upd[n_i], 0.0)))
        return outs

    def sweep(q_ref, k_ref, v_ref, o_ref):
        n = q_ref.shape[1] // c

        def body(i, carry):
            rows = [pl.ds(pl.multiple_of(idx * c, c), c) for idx in (i, n - 1 - i)]
            items = [(q_ref[0, r, :], k_ref[0, r, :], v_ref[0, r, :], st_ref[...], d)
                     for d, (r, st_ref) in enumerate(zip(rows, (sf_ref, sb_ref)))]
            (o_f, st_f), (o_b, st_b) = chunks(items)
            sf_ref[...] = st_f
            sb_ref[...] = st_b
            o_ref[0, rows[0], :] += o_f
            o_ref[0, rows[1], :] += o_b
            return carry

        lax.fori_loop(0, n, body, 0)

    sweep(qc_ref, kc_ref, vc_ref, oc_ref)
    sweep(ql_ref, kl_ref, vl_ref, ol_ref)


def _ret_call(lg, pc, plat):
    b = pc["rq"].shape[0]
    args = [pc["rq"], pc["rk"], pc["rv"], plat["rq"], plat["rk"], plat["rv"]]
    lc, ll = pc["rq"].shape[1], plat["rq"].shape[1]
    return pl.pallas_call(
        _ret_kernel,
        grid=(b,),
        in_specs=[_const_spec(lg.shape)] + [_seq_spec(a) for a in args],
        out_specs=[pl.BlockSpec((1, lc, GROUP_W), lambda i: (i, 0, 0)),
                   pl.BlockSpec((1, ll, GROUP_W), lambda i: (i, 0, 0))],
        out_shape=[jax.ShapeDtypeStruct((b, lc, GROUP_W), F32), jax.ShapeDtypeStruct((b, ll, GROUP_W), F32)],
        scratch_shapes=[pltpu.VMEM((GROUP_W, 128), F32), pltpu.VMEM((GROUP_W, 128), F32),
                        pltpu.VMEM((2, N_H * RET_CHUNK, RET_CHUNK), F32)],
        compiler_params=_params("parallel"),
        name="ret_scan",
    )(lg, *args)


def _mix_out_kernel(x_ref, mod_ref, ng_ref, do_ref, mo_ref, go_ref, gr_ref, ro_ref, rg_ref,
                    gg_ref, rgn_ref, wo_ref, x1_ref, h2_ref):
    gmat = _group_mean_matrix(GROUP_W, HEAD_DIM)
    gl = _group_norm(go_ref[0], gg_ref[...], gmat) * _silu(gr_ref[0].astype(F32))
    rt = _group_norm(ro_ref[0], rgn_ref[...], gmat) * _silu(rg_ref[0].astype(F32))
    ml = (_dot(do_ref[0], wo_ref[0:256]) + _dot(mo_ref[0], wo_ref[256:512])
          + _dot(gl.astype(BF16), wo_ref[512:768]) + _dot(rt.astype(BF16), wo_ref[768:1024]))
    x1 = x_ref[0] + mod_ref[0, 2:3] * _rms(ml, ng_ref[1:2])
    x1_ref[0] = x1
    h2_ref[0] = (_rms(x1, ng_ref[2:3]) * (1.0 + mod_ref[0, 4:5]) + mod_ref[0, 3:4]).astype(BF16)


def _mod_spec(mod, d):
    if mod.shape[0] > 1:
        return pl.BlockSpec((1, 6, d), lambda i, j: (i, 0, 0))
    return pl.BlockSpec((1, 6, d), lambda i, j: (0, 0, 0))


def _mix_out_call(x, mod, ng, d_o, m_o, g_o, g_r, r_o, r_g, lw, tm):
    b, n, d = x.shape

    def tok(w):
        return pl.BlockSpec((1, tm, w), lambda i, j: (i, j, 0))

    return pl.pallas_call(
        _mix_out_kernel,
        grid=(b, n // tm),
        in_specs=[tok(d), _mod_spec(mod, d), _const_spec(ng.shape)] + [tok(GROUP_W)] * 6
                 + [_const_spec(lw["gla_g"].shape), _const_spec(lw["ret_g"].shape), _const_spec(lw["w_out"].shape)],
        out_specs=[tok(d), tok(d)],
        out_shape=[jax.ShapeDtypeStruct((b, n, d), F32), jax.ShapeDtypeStruct((b, n, d), BF16)],
        compiler_params=_params("parallel", "parallel"),
        name="mix_out",
    )(x, mod, ng, d_o, m_o, g_o, g_r, r_o, r_g, lw["gla_g"], lw["ret_g"], lw["w_out"])


def _ffn_kernel(h_ref, x1_ref, mod_ref, ng_ref, wg_ref, wu_ref, wo_ref, x2_ref):
    h = h_ref[0]
    a = (_silu(_dot(h, wg_ref[...])) * _dot(h, wu_ref[...])).astype(BF16)
    y = _dot(a, wo_ref[...])
    x2_ref[0] = x1_ref[0] + mod_ref[0, 5:6] * _rms(y, ng_ref[3:4])


def _ffn_call(h2, x1, mod, ng, lw, tm):
    b, n, d = x1.shape

    def tok(w):
        return pl.BlockSpec((1, tm, w), lambda i, j: (i, j, 0))

    return pl.pallas_call(
        _ffn_kernel,
        grid=(b, n // tm),
        in_specs=[tok(d), tok(d), _mod_spec(mod, d), _const_spec(ng.shape),
                  _const_spec(lw["ffn_g"].shape), _const_spec(lw["ffn_u"].shape), _const_spec(lw["ffn_o"].shape)],
        out_specs=tok(d),
        out_shape=jax.ShapeDtypeStruct((b, n, d), F32),
        compiler_params=_params("parallel", "parallel"),
        name="ffn",
    )(h2, x1, mod, ng, lw["ffn_g"], lw["ffn_u"], lw["ffn_o"])


def _pack_layer(l, w_in, diff_norm, mla_q_norm, mla_kv_norm, mla_w_uq, mla_w_ukv, gla_w_gate, gla_b_gate,
                gla_norm, ret_decay, ret_norm, w_out, ffn_w_in, ffn_w_out):
    d = w_in.shape[1]
    w = w_in[l]
    sizes = (256, 256, 256, 256, 128, 32, 128, 128, 256, 256, 16, 16, 128, 128, 256, 256)
    offs = np.concatenate([[0], np.cumsum(sizes)])
    seg = [w[:, offs[i]:offs[i + 1]] for i in range(len(sizes))]
    z = lambda n: jnp.zeros((d, n), w.dtype)
    misc = jnp.concatenate([seg[10], seg[11], z(32), seg[5], z(32)], axis=1)
    wm = jnp.concatenate(seg[0:5] + [misc] + seg[6:10] + seg[12:16], axis=1).astype(BF16)

    uq = mla_w_uq[l].reshape(-1, N_H, MLA_NOPE + MLA_ROPE)
    uq = jnp.pad(uq, ((0, 0), (0, 0), (0, MLA_SLOT - MLA_NOPE - MLA_ROPE))).reshape(-1, N_H * MLA_SLOT)
    ukv = mla_w_ukv[l].reshape(-1, N_H, MLA_NOPE + HEAD_DIM)
    uk = jnp.pad(ukv[:, :, :MLA_NOPE], ((0, 0), (0, 0), (0, MLA_SLOT - MLA_NOPE))).reshape(-1, N_H * MLA_SLOT)
    uv = ukv[:, :, MLA_NOPE:].reshape(-1, GROUP_W)

    wgf = jnp.zeros((128, 128), F32).at[0:GLA_GATE_RANK].set(gla_w_gate[l, 0]).astype(BF16)
    wgb = jnp.zeros((128, 128), F32).at[GLA_GATE_RANK:2 * GLA_GATE_RANK].set(gla_w_gate[l, 1]).astype(BF16)
    hid = ffn_w_out.shape[1]
    return {
        "wm": wm, "qg": mla_q_norm[l][None], "kvg": mla_kv_norm[l][None],
        "wuq": uq.astype(BF16), "wuk": uk.astype(BF16), "wuv": uv.astype(BF16),
        "wgf": wgf, "wgb": wgb, "bg": gla_b_gate[l],
        "diff_g": jnp.tile(diff_norm[l], N_H)[None], "gla_g": jnp.tile(gla_norm[l], N_H)[None],
        "ret_g": jnp.tile(ret_norm[l], N_H)[None],
        "ret_lg": jnp.repeat(ret_decay[l], RET_DK, axis=1),
        "w_out": w_out[l].astype(BF16),
        "ffn_g": ffn_w_in[l][:, :hid].astype(BF16), "ffn_u": ffn_w_in[l][:, hid:].astype(BF16),
        "ffn_o": ffn_w_out[l].astype(BF16),
    }


def _rope_tables(seq):
    rows = seq // GRID_W
    row = jnp.repeat(jnp.arange(rows, dtype=F32), GRID_W)
    col = jnp.tile(jnp.arange(GRID_W, dtype=F32), rows)
    n_freq = ROT_DIM // 4
    freqs = ROPE_THETA ** (-jnp.arange(n_freq, dtype=F32) / n_freq)
    ang = jnp.concatenate([row[:, None] * freqs, col[:, None] * freqs], axis=-1)
    cos, sin = jnp.cos(ang), jnp.sin(ang)
    c32 = jnp.concatenate([cos, cos], axis=-1)
    s32 = jnp.concatenate([-sin, sin], axis=-1)
    one, zero = jnp.ones((seq, 1), F32), jnp.zeros((seq, 1), F32)
    slot_c = jnp.concatenate([jnp.tile(one, (1, MLA_NOPE)), c32, jnp.tile(one, (1, MLA_SLOT - MLA_NOPE - MLA_ROPE))], -1)
    slot_s = jnp.concatenate([jnp.tile(zero, (1, MLA_NOPE)), s32, jnp.tile(zero, (1, MLA_SLOT - MLA_NOPE - MLA_ROPE))], -1)
    return (jnp.tile(c32, (1, 256 // ROT_DIM)), jnp.tile(s32, (1, 256 // ROT_DIM)),
            jnp.tile(slot_c, (1, N_H)), jnp.tile(slot_s, (1, N_H)))


def _tile(n, want):
    t = min(n, want)
    while n % t:
        t //= 2
    return t


def kernel(x, c, ctx, c_ctx, ada_w, ada_b, norm_g, w_in, diff_lam, diff_norm, mla_q_norm, mla_kv_norm,
           mla_w_uq, mla_w_ukv, gla_w_gate, gla_b_gate, gla_norm, ret_decay, ret_norm, w_out, ffn_w_in, ffn_w_out):
    b, seq, d = x.shape
    lc = ctx.shape[1]
    depth = ada_w.shape[0]
    rope_tabs = _rope_tables(seq)

    pad = (-(b + 1)) % 8
    s_all = jnp.concatenate([c, c_ctx[None], jnp.zeros((pad, d), F32)], axis=0)
    mods = _ada_call(s_all, ada_w, ada_b)

    tm_l, tm_c = _tile(seq, 512), _tile(lc, 256)
    tq_c = _tile(lc, 256)
    tq_d, tk_d = _tile(seq, 256), _tile(seq, 1024)
    tq_m, tk_m = _tile(seq, 512), _tile(seq, 1024)

    xl, xc = x, ctx
    for l in range(depth):
        with_ctx = l < depth - 1
        lambda_init = 0.8 - 0.6 * math.exp(-0.3 * l)
        lw = _pack_layer(l, w_in, diff_norm, mla_q_norm, mla_kv_norm, mla_w_uq, mla_w_ukv, gla_w_gate,
                         gla_b_gate, gla_norm, ret_decay, ret_norm, w_out, ffn_w_in, ffn_w_out)
        mod_l = mods[l, :b].reshape(b, 6, d)
        mod_c = mods[l, b:b + 1].reshape(1, 6, d)
        ng = norm_g[l]

        pc = _proj_call(xc, mod_c, ng, lw, None, tm_c)
        pt = _proj_call(xl, mod_l, ng, lw, rope_tabs, tm_l)

        dargs = (diff_lam[l], lw["diff_g"], lambda_init)
        d_l = _attn_call(pt["dq"], pc["dk"], pc["dv"], pt["dk"], pt["dv"], dargs, tq_d, tk_d, "diff_attn")
        m_l = _attn_call(pt["mq"], pc["mk"], pc["mv"], pt["mk"], pt["mv"], None, tq_m, tk_m, "mla_attn")
        g_c, g_l = _gla_call(pc, pt)
        r_c, r_l = _ret_call(lw["ret_lg"], pc, pt)

        x1, h2 = _mix_out_call(xl, mod_l, ng, d_l, m_l, g_l, pt["gr"], r_l, pt["rg"], lw, tm_l)
        xl = _ffn_call(h2, x1, mod_l, ng, lw, tm_l)
        if with_ctx:
            d_c = _attn_call(pc["dq"], pc["dk"], pc["dv"], None, None, dargs, tq_c, tk_d, "diff_attn_ctx")
            m_c = _attn_call(pc["mq"], pc["mk"], pc["mv"], None, None, None, tq_c, tk_m, "mla_attn_ctx")
            x1c, h2c = _mix_out_call(xc, mod_c, ng, d_c, m_c, g_c, pc["gr"], r_c, pc["rg"], lw, tm_c)
            xc = _ffn_call(h2c, x1c, mod_c, ng, lw, tm_c)
    return xl
```
